```python
import math
import jax
import jax.numpy as jnp
from jax import lax
import numpy as np

D_MODEL = 1024
BATCH = 4
SEQ = 4096
DEPTH = 4

GRID_W = 64
CTX_LEN = 256
N_EVEN = (DEPTH + 1) // 2
N_ODD = DEPTH // 2

FOURIER_GROUPS = 4
FOURIER_DIM = 128
FOURIER_WIDTH = FOURIER_GROUPS * FOURIER_DIM
NA_HEADS = 8
NA_HEAD_DIM = 64
NA_WIDTH = NA_HEADS * NA_HEAD_DIM
NA_KH_MAX = 8
NA_KW = 16
AB_IN = FOURIER_WIDTH + 3 * NA_WIDTH
AB_OUT = FOURIER_WIDTH + NA_WIDTH
DIFF_HEADS = 8
DIFF_HEAD_DIM = 64
DIFF_V_DIM = 2 * DIFF_HEAD_DIM
DIFF_QK_WIDTH = DIFF_HEADS * 2 * DIFF_HEAD_DIM
DIFF_V_WIDTH = DIFF_HEADS * DIFF_V_DIM
ROPE_THETA = 10000.0
Q_BLOCK = 128
MLP_HIDDEN = 4 * D_MODEL
NORM_EPS = 1e-6
SUBLN_EPS = 1e-5
NEG_INF = -1e30

kernel_name = 'hybrid_fourier_natten_diffattn_dit'


def rms_norm(x, g, eps=NORM_EPS):
    xf = x.astype(jnp.float32)
    y = xf * lax.rsqrt(jnp.mean(xf * xf, axis=-1, keepdims=True) + eps)
    return (y * g.astype(jnp.float32)).astype(x.dtype)


def modulate(x, g, shift, scale):
    return rms_norm(x, g) * (1.0 + scale) + shift


def squared_relu_mlp(h, w_in, w_out):
    return jnp.square(jax.nn.relu(h @ w_in)) @ w_out


def to_heads(t, n_heads):
    b, l, _ = t.shape
    return t.reshape(b, l, n_heads, -1).transpose(0, 2, 1, 3)


def dense_attention(q, k, v):
    s = jnp.einsum('bhqd,bhkd->bhqk', q, k).astype(jnp.float32) * (q.shape[-1] ** -0.5)
    p = jax.nn.softmax(s, axis=-1).astype(v.dtype)
    return jnp.einsum('bhqk,bhkd->bhqd', p, v)


def fourier_mix(f):
    b, l, _ = f.shape
    z = f.reshape(b, l, FOURIER_GROUPS, FOURIER_DIM).astype(jnp.float32)
    y = jnp.fft.fft2(z, axes=(1, 3), norm='ortho').real
    return y.reshape(b, l, FOURIER_WIDTH).astype(f.dtype)


def neighbourhood_attention(q, k, v, kc, vc, rpb):
    b, n, _ = q.shape
    h, d, w = NA_HEADS, NA_HEAD_DIM, GRID_W
    rows_n = n // w
    kh = min(NA_KH_MAX, rows_n)

    def grid(t):
        return t.reshape(b, rows_n, w, h, d).transpose(0, 3, 1, 2, 4)

    qg = grid(q) * (d ** -0.5)
    kg, vg = grid(k), grid(v)
    kch, vch = to_heads(kc, h), to_heads(vc, h)

    r = jnp.arange(rows_n)
    col = jnp.arange(w)
    row_start = jnp.clip(r - kh // 2, 0, rows_n - kh)
    key_rows = row_start[:, None] + jnp.arange(kh)
    k_win = kg[:, :, key_rows]
    v_win = vg[:, :, key_rows]

    col_start = jnp.clip(col - NA_KW // 2, 0, w - NA_KW)
    in_win = (col[None, :] >= col_start[:, None]) & (col[None, :] < col_start[:, None] + NA_KW)
    dr = key_rows - r[:, None] + (NA_KH_MAX - 1)
    dc = jnp.clip(col[None, :] - col[:, None], -(NA_KW - 1), NA_KW - 1) + (NA_KW - 1)
    bias = rpb[:, dr[:, None, :, None], dc[None, :, None, :]].astype(jnp.float32)

    s_win = jnp.einsum('bhrqd,bhrjkd->bhrqjk', qg, k_win).astype(jnp.float32) + bias
    s_win = jnp.where(in_win[:, None, :], s_win, NEG_INF)
    s_ctx = jnp.einsum('bhrqd,bhkd->bhrqk', qg, kch).astype(jnp.float32)
    s = jnp.concatenate([s_win.reshape(b, h, rows_n, w, kh * w), s_ctx], axis=-1)
    p = jax.nn.softmax(s, axis=-1).astype(v.dtype)
    p_win = p[..., :kh * w].reshape(b, h, rows_n, w, kh, w)
    p_ctx = p[..., kh * w:]
    o = (jnp.einsum('bhrqjk,bhrjkd->bhrqd', p_win, v_win)
         + jnp.einsum('bhrqk,bhkd->bhrqd', p_ctx, vch))
    return o.transpose(0, 2, 3, 1, 4).reshape(b, n, NA_WIDTH)


def fourier_na_mixer(hl, hc, w_in, w_out, rpb, with_ctx_out):
    splits = [FOURIER_WIDTH, FOURIER_WIDTH + NA_WIDTH, FOURIER_WIDTH + 2 * NA_WIDTH]
    fl, ql, kl, vl = jnp.split(hl @ w_in, splits, axis=-1)
    fc, qc, kc, vc = jnp.split(hc @ w_in, splits, axis=-1)
    yl = jnp.concatenate([fourier_mix(fl), neighbourhood_attention(ql, kl, vl, kc, vc, rpb)], axis=-1) @ w_out
    if not with_ctx_out:
        return yl, None
    oc = dense_attention(to_heads(qc, NA_HEADS), to_heads(kc, NA_HEADS), to_heads(vc, NA_HEADS))
    oc = oc.transpose(0, 2, 1, 3).reshape(hc.shape[0], hc.shape[1], NA_WIDTH)
    yc = jnp.concatenate([fourier_mix(fc), oc], axis=-1) @ w_out
    return yl, yc


def axial_rope(t, rows, cols):
    half = t.shape[-1] // 2
    nf = half // 2
    inv_freq = ROPE_THETA ** (-jnp.arange(nf, dtype=jnp.float32) / nf)

    def rotate(u, pos):
        ang = pos.astype(jnp.float32)[:, None] * inv_freq
        cos = jnp.cos(ang)[:, None, None, :]
        sin = jnp.sin(ang)[:, None, None, :]
        u = u.astype(jnp.float32)
        u1, u2 = u[..., :nf], u[..., nf:]
        return jnp.concatenate([u1 * cos - u2 * sin, u1 * sin + u2 * cos], axis=-1)

    out = jnp.concatenate([rotate(t[..., :half], rows), rotate(t[..., half:], cols)], axis=-1)
    return out.astype(t.dtype)


def diff_attn_core(q, k, v, lam):
    s = jnp.einsum('bhiqd,bhikd->bhiqk', q, k).astype(jnp.float32) * (q.shape[-1] ** -0.5)
    p = jax.nn.softmax(s, axis=-1)
    a = (p[:, :, 0] - lam * p[:, :, 1]).astype(v.dtype)
    return jnp.einsum('bhqk,bhkd->bhqd', a, v)


def diff_attention_mixer(hl, hc, w_qkv, w_out, lq1, lk1, lq2, lk2, subln_g, lam_init, rows, cols, with_ctx_out):
    h, d, dv = DIFF_HEADS, DIFF_HEAD_DIM, DIFF_V_DIM

    def project(x_):
        b_, l_, _ = x_.shape
        q, k, v = jnp.split(x_ @ w_qkv, [DIFF_QK_WIDTH, 2 * DIFF_QK_WIDTH], axis=-1)
        return q.reshape(b_, l_, h, 2, d), k.reshape(b_, l_, h, 2, d), v.reshape(b_, l_, h, dv)

    ql, kl, vl = project(hl)
    qc, kc, vc = project(hc)
    ql, kl = axial_rope(ql, rows, cols), axial_rope(kl, rows, cols)
    lam = (jnp.exp(jnp.sum((lq1 * lk1).astype(jnp.float32)))
           - jnp.exp(jnp.sum((lq2 * lk2).astype(jnp.float32))) + lam_init)

    kc_h = kc.transpose(0, 2, 3, 1, 4)
    vc_h = vc.transpose(0, 2, 1, 3)
    k_all = jnp.concatenate([kc_h, kl.transpose(0, 2, 3, 1, 4)], axis=3)
    v_all = jnp.concatenate([vc_h, vl.transpose(0, 2, 1, 3)], axis=2)

    b, n = hl.shape[0], hl.shape[1]
    nb = n // Q_BLOCK
    q_blocks = ql.reshape(b, nb, Q_BLOCK, h, 2, d).transpose(1, 0, 3, 4, 2, 5)
    ol = lax.map(lambda qb: diff_attn_core(qb, k_all, v_all, lam), q_blocks)
    ol = ol.transpose(1, 0, 3, 2, 4).reshape(b, n, h, dv)

    def head_out(o):
        o = rms_norm(o, subln_g, SUBLN_EPS) * (1.0 - lam_init)
        return o.reshape(o.shape[0], o.shape[1], DIFF_V_WIDTH) @ w_out

    yl = head_out(ol)
    if not with_ctx_out:
        return yl, None
    oc = diff_attn_core(qc.transpose(0, 2, 3, 1, 4), kc_h, vc_h, lam).transpose(0, 2, 1, 3)
    return yl, head_out(oc)


def setup_inputs(seed: int = 0) -> dict:
    key = jax.random.key(seed)
    ks = jax.random.split(key, 22)
    D = D_MODEL

    def nrm(k, shape, scale):
        return jax.random.normal(k, shape, jnp.float32) * scale

    return {
        'x': nrm(ks[0], (BATCH, SEQ, D), 1.0),
        'c': nrm(ks[1], (BATCH, D), 1.0),
        'ctx': nrm(ks[2], (BATCH, CTX_LEN, D), 1.0),
        'c_ctx': nrm(ks[3], (D,), 1.0),
        'w_mod': nrm(ks[4], (DEPTH, D, 6 * D), 0.5 * D ** -0.5),
        'b_mod': nrm(ks[5], (DEPTH, 6 * D), 0.02),
        'norm_mix_g': 1.0 + nrm(ks[6], (DEPTH, D), 0.02),
        'norm_mlp_g': 1.0 + nrm(ks[7], (DEPTH, D), 0.02),
        'w_mlp_in': nrm(ks[8], (DEPTH, D, MLP_HIDDEN), D ** -0.5),
        'w_mlp_out': nrm(ks[9], (DEPTH, MLP_HIDDEN, D), MLP_HIDDEN ** -0.5),
        'w_in_ab': nrm(ks[10], (N_EVEN, D, AB_IN), D ** -0.5),
        'w_out_ab': nrm(ks[11], (N_EVEN, AB_OUT, D), AB_OUT ** -0.5),
        'na_rpb': nrm(ks[12], (N_EVEN, NA_HEADS, 2 * NA_KH_MAX - 1, 2 * NA_KW - 1), 0.1),
        'w_qkv_diff': nrm(ks[13], (N_ODD, D, 2 * DIFF_QK_WIDTH + DIFF_V_WIDTH), D ** -0.5),
        'w_out_diff': nrm(ks[14], (N_ODD, DIFF_V_WIDTH, D), DIFF_V_WIDTH ** -0.5),
        'diff_lq1': nrm(ks[15], (N_ODD, DIFF_HEAD_DIM), 0.1),
        'diff_lk1': nrm(ks[16], (N_ODD, DIFF_HEAD_DIM), 0.1),
        'diff_lq2': nrm(ks[17], (N_ODD, DIFF_HEAD_DIM), 0.1),
        'diff_lk2': nrm(ks[18], (N_ODD, DIFF_HEAD_DIM), 0.1),
        'diff_subln_g': 1.0 + nrm(ks[19], (N_ODD, DIFF_V_DIM), 0.02),
        'final_norm_g': 1.0 + nrm(ks[20], (D,), 0.02),
    }


def reference(x, c, ctx, c_ctx, w_mod, b_mod, norm_mix_g, norm_mlp_g, w_mlp_in, w_mlp_out,
              w_in_ab, w_out_ab, na_rpb, w_qkv_diff, w_out_diff, diff_lq1, diff_lk1, diff_lq2, diff_lk2,
              diff_subln_g, final_norm_g):
    n = x.shape[1]
    t = jnp.arange(n)
    rows, cols = t // GRID_W, t % GRID_W
    c_act = jax.nn.silu(c)
    cc_act = jax.nn.silu(c_ctx)
    xl, xc = x, ctx
    for i in range(DEPTH):
        last = i == DEPTH - 1
        ml = jnp.split((c_act @ w_mod[i] + b_mod[i])[:, None, :], 6, axis=-1)
        mc = jnp.split(cc_act @ w_mod[i] + b_mod[i], 6, axis=-1)
        hl = modulate(xl, norm_mix_g[i], ml[0], ml[1])
        hc = modulate(xc, norm_mix_g[i], mc[0], mc[1])
        j = i // 2
        if i % 2 == 0:
            yl, yc = fourier_na_mixer(hl, hc, w_in_ab[j], w_out_ab[j], na_rpb[j], not last)
        else:
            lam_init = 0.8 - 0.6 * math.exp(-0.3 * i)
            yl, yc = diff_attention_mixer(hl, hc, w_qkv_diff[j], w_out_diff[j], diff_lq1[j], diff_lk1[j],
                                          diff_lq2[j], diff_lk2[j], diff_subln_g[j], lam_init,
                                          rows, cols, not last)
        xl = xl + ml[2] * yl
        xl = xl + ml[5] * squared_relu_mlp(modulate(xl, norm_mlp_g[i], ml[3], ml[4]), w_mlp_in[i], w_mlp_out[i])
        if not last:
            xc = xc + mc[2] * yc
            xc = xc + mc[5] * squared_relu_mlp(modulate(xc, norm_mlp_g[i], mc[3], mc[4]), w_mlp_in[i], w_mlp_out[i])
    return rms_norm(xl, final_norm_g)
```

```python
import functools
import math

import numpy as np
import jax
import jax.numpy as jnp
from jax import lax
from jax.experimental import pallas as pl
from jax.experimental.pallas import tpu as pltpu

F32 = jnp.float32
BF16 = jnp.bfloat16

D = 1024
B = 4
N = 4096
CTX = 256
DEPTH = 4
GW = 64
GH = N // GW
FW = 512
FG = 4
FD = 128
NAW = 512
HD = 64
NA_KH = 8
NA_KW = 16
MLP_H = 4 * D
NORM_EPS = 1e-6
SUBLN_EPS = 1e-5
NEG = -1e30
ROPE_THETA = 10000.0
LOG2E = 1.4426950408889634
Q_SCALE = HD ** -0.5 * LOG2E

TM = 512
LAT = B * N
ROWS = LAT + B * CTX
NT = ROWS // TM
NT_LAT = LAT // TM
TILES_PER_BATCH = N // TM
TQ = 512
QROWS = TQ // GW
NA_WIN_ROWS = 16
NA_WIN = NA_WIN_ROWS * GW
TMF = 256
DIFF_SUM_ROWS = 16
DIFF_CK = 256

VMEM_LIMIT = 60 * 1024 * 1024


def _cparams(sem):
    return pltpu.CompilerParams(dimension_semantics=sem, vmem_limit_bytes=VMEM_LIMIT)


def _mod_kernel(cin_ref, w_ref, b_ref, o_ref):
    cin = cin_ref[...]
    act = cin * jax.nn.sigmoid(cin)
    o_ref[...] = jnp.dot(act, w_ref[...], preferred_element_type=F32,
                         precision=lax.Precision.HIGHEST) + b_ref[...]


def _modulation(cin, w_mod, b_mod):
    tn = 1536
    return pl.pallas_call(
        _mod_kernel,
        grid=(DEPTH, 6 * D // tn),
        in_specs=[
            pl.BlockSpec((8, D), lambda l, j: (0, 0)),
            pl.BlockSpec((None, D, tn), lambda l, j: (l, 0, j)),
            pl.BlockSpec((None, 1, tn), lambda l, j: (l, 0, j)),
        ],
        out_specs=pl.BlockSpec((None, 8, tn), lambda l, j: (l, 0, j)),
        out_shape=jax.ShapeDtypeStruct((DEPTH, 8, 6 * D), F32),
        compiler_params=_cparams(("arbitrary", "arbitrary")),
        name="modulation",
    )(cin, w_mod, b_mod.reshape(DEPTH, 1, 6 * D))


def _norm_mod(x, g, shift, scale):
    ms = jnp.mean(x * x, axis=-1, keepdims=True)
    y = x * lax.rsqrt(ms + NORM_EPS) * g
    return y * (1.0 + scale) + shift


def _dot(a, b):
    return jnp.dot(a, b, preferred_element_type=F32)


def _dot_t(a, b):
    return lax.dot_general(a, b, (((1,), (1,)), ((), ())), preferred_element_type=F32)


def _softmax_pv(s, v_ext):
    m = jnp.max(s, axis=1, keepdims=True)
    p = jnp.exp2((s - m).astype(BF16))
    return _dot(p, v_ext)


def _lane_lo():
    return lax.broadcasted_iota(jnp.int32, (1, 2 * HD), 1) < HD


def _proj_even_kernel(x_ref, mod_ref, g_ref, w_ref, cs_ref, fcs_ref, qkv_ref):
    h = _norm_mod(x_ref[...], g_ref[...], mod_ref[0:1, :], mod_ref[1:2, :]).astype(BF16)
    p = _dot(h, w_ref[...])
    f = p[:, :FW].astype(BF16)
    fcs_ref[...] = _dot(f, cs_ref[...]).astype(BF16)
    qkv_ref[:, :NAW] = (p[:, FW:FW + NAW] * Q_SCALE).astype(BF16)
    qkv_ref[:, NAW:] = p[:, FW + NAW:].astype(BF16)


def _proj_even(xs, mods_l, g, w, cs):
    nout = FW + 3 * NAW
    return pl.pallas_call(
        _proj_even_kernel,
        grid=(NT,),
        in_specs=[
            pl.BlockSpec((TM, D), lambda t: (t, 0)),
            pl.BlockSpec((None, 6, D), lambda t: (t // TILES_PER_BATCH, 0, 0)),
            pl.BlockSpec((1, D), lambda t: (0, 0)),
            pl.BlockSpec((D, nout), lambda t: (0, 0)),
            pl.BlockSpec((FW, 2 * FW), lambda t: (0, 0)),
        ],
        out_specs=[
            pl.BlockSpec((TM, 2 * FW), lambda t: (t, 0)),
            pl.BlockSpec((TM, 3 * NAW), lambda t: (t, 0)),
        ],
        out_shape=[
            jax.ShapeDtypeStruct((ROWS, 2 * FW), BF16),
            jax.ShapeDtypeStruct((ROWS, 3 * NAW), BF16),
        ],
        compiler_params=_cparams(("arbitrary",)),
        name="proj_even",
    )(xs, mods_l, g, w, cs)


def _proj_odd_kernel(x_ref, mod_ref, g_ref, w_ref, cos_ref, sa_ref, sb_ref, qkv_ref):
    h = _norm_mod(x_ref[...], g_ref[...], mod_ref[0:1, :], mod_ref[1:2, :]).astype(BF16)
    p = _dot(h, w_ref[...])
    cos = cos_ref[...]
    sa = sa_ref[...]
    sb = sb_ref[...]
    for c in range(2 * D // 128):
        xc = p[:, c * 128:(c + 1) * 128]
        r = xc * cos + pltpu.roll(xc, 112, 1) * sa + pltpu.roll(xc, 16, 1) * sb
        if c < D // 128:
            r = r * Q_SCALE
        qkv_ref[:, c * 128:(c + 1) * 128] = r.astype(BF16)
    qkv_ref[:, 2 * D:] = p[:, 2 * D:].astype(BF16)


def _proj_odd(xs, mods_l, g, w, cos, sa, sb):
    nout = 3 * D

    def rope_idx(t):
        return (jnp.where(t < NT_LAT, t % TILES_PER_BATCH, TILES_PER_BATCH), 0)

    return pl.pallas_call(
        _proj_odd_kernel,
        grid=(NT,),
        in_specs=[
            pl.BlockSpec((TM, D), lambda t: (t, 0)),
            pl.BlockSpec((None, 6, D), lambda t: (t // TILES_PER_BATCH, 0, 0)),
            pl.BlockSpec((1, D), lambda t: (0, 0)),
            pl.BlockSpec((D, nout), lambda t: (0, 0)),
            pl.BlockSpec((TM, 128), rope_idx),
            pl.BlockSpec((TM, 128), rope_idx),
            pl.BlockSpec((TM, 128), rope_idx),
        ],
        out_specs=pl.BlockSpec((TM, nout), lambda t: (t, 0)),
        out_shape=jax.ShapeDtypeStruct((ROWS, nout), BF16),
        compiler_params=_cparams(("arbitrary",)),
        name="proj_odd",
    )(xs, mods_l, g, w, cos, sa, sb)


def _fourier_kernel(x_ref, tc_ref, ts_ref, ca_ref, sa_ref, o_ref):
    tc = tc_ref[...]
    ts = ts_ref[...]
    ca = ca_ref[...]
    sa = sa_ref[...]
    mc = (tc * ca - ts * sa).astype(BF16)
    ms = (ts * ca + tc * sa).astype(BF16)
    y = _dot(mc, x_ref[:, :FW]) - _dot(ms, x_ref[:, FW:])
    o_ref[...] = y.astype(BF16)


def _fourier(fcs, tc, ts, ca, sa):
    nti = N // TMF
    return pl.pallas_call(
        _fourier_kernel,
        grid=(B, nti),
        in_specs=[
            pl.BlockSpec((N, 2 * FW), lambda b, i: (b, 0)),
            pl.BlockSpec((TMF, N), lambda b, i: (0, 0)),
            pl.BlockSpec((TMF, N), lambda b, i: (0, 0)),
            pl.BlockSpec((None, 1, N), lambda b, i: (i, 0, 0)),
            pl.BlockSpec((None, 1, N), lambda b, i: (i, 0, 0)),
        ],
        out_specs=pl.BlockSpec((TMF, FW), lambda b, i: (b * nti + i, 0)),
        out_shape=jax.ShapeDtypeStruct((ROWS, D), BF16),
        compiler_params=_cparams(("arbitrary", "arbitrary")),
        name="fourier",
    )(fcs, tc, ts, ca, sa)


def _fourier_ctx_kernel(x_ref, m_ref, a_ref, o_ref):
    del a_ref
    y = _dot(m_ref[:, :CTX], x_ref[:, :FW]) - _dot(m_ref[:, CTX:], x_ref[:, FW:])
    o_ref[...] = y.astype(BF16)


def _fourier_ctx(fcs, mctx, a):
    return pl.pallas_call(
        _fourier_ctx_kernel,
        grid=(B,),
        in_specs=[
            pl.BlockSpec((CTX, 2 * FW), lambda b: (LAT // CTX + b, 0)),
            pl.BlockSpec((CTX, 2 * CTX), lambda b: (0, 0)),
            pl.BlockSpec(memory_space=pl.ANY),
        ],
        out_specs=pl.BlockSpec((CTX, FW), lambda b: (LAT // CTX + b, 0)),
        out_shape=jax.ShapeDtypeStruct((ROWS, D), BF16),
        input_output_aliases={2: 0},
        compiler_params=_cparams(("arbitrary",)),
        name="fourier_ctx",
    )(fcs, mctx, a)


def _na_bias_kernel(rpb_ref, o_ref):
    j = pl.program_id(0)
    p = pl.program_id(1)
    cq = lax.broadcasted_iota(jnp.int32, (GW, 2 * GW), 0)
    lane = lax.broadcasted_iota(jnp.int32, (GW, 2 * GW), 1)
    ck = lane & (GW - 1)
    left = lane < GW
    cs = jnp.clip(cq - NA_KW // 2, 0, GW - NA_KW)
    col_ok = (ck >= cs) & (ck < cs + NA_KW)
    bidx = ck - cq + (NA_KW - 1)
    zero = jnp.zeros((GW, 2 * GW), F32)
    neg = jnp.full((GW, 2 * GW), NEG, F32)
    neg_l = jnp.where(left, neg, zero)
    neg_r = jnp.where(left, zero, neg)
    nb = 2 * NA_KW - 1
    na = 2 * NA_KH - 1
    for hh in range(2):
        base = ((j * 8 + 2 * p + hh) * na) * nb
        halves_l, halves_r = [], []
        for a in range(na):
            r = zero
            for b in range(nb):
                r = jnp.where(bidx == b, rpb_ref[base + a * nb + b], r)
            r = jnp.where(col_ok, r * LOG2E, neg)
            halves_l.append(jnp.where(left, r, zero))
            halves_r.append(jnp.where(left, zero, r))
        for cls, r0 in enumerate((0, QROWS, GH - QROWS)):
            start = min(max(r0 - NA_KH // 2, 0), GH - NA_WIN_ROWS)
            for rql in range(QROWS):
                rq = r0 + rql
                rs = min(max(rq - NA_KH // 2, 0), GH - NA_KH)
                for m in range(NA_WIN_ROWS // 2):
                    rk0 = start + 2 * m
                    rk1 = rk0 + 1
                    tl = halves_l[rk0 - rq + NA_KH - 1] if rs <= rk0 < rs + NA_KH else neg_l
                    tr = halves_r[rk1 - rq + NA_KH - 1] if rs <= rk1 < rs + NA_KH else neg_r
                    o_ref[cls, hh, rql * GW:(rql + 1) * GW, m * 128:(m + 1) * 128] = (tl + tr).astype(BF16)


def _na_bias(rpb_flat):
    n_even = (DEPTH + 1) // 2
    return pl.pallas_call(
        _na_bias_kernel,
        grid=(n_even, 4),
        in_specs=[pl.BlockSpec(memory_space=pltpu.SMEM)],
        out_specs=pl.BlockSpec((None, None, 3, 2, TQ, NA_WIN), lambda j, p: (j, p, 0, 0, 0, 0)),
        out_shape=jax.ShapeDtypeStruct((n_even, 4, 3, 2, TQ, NA_WIN), BF16),
        compiler_params=_cparams(("arbitrary", "arbitrary")),
        name="na_bias",
    )(rpb_flat)


def _na_kernel(q_ref, k_ref, v_ref, kc_ref, vc_ref, bias_ref, a_ref, o_ref):
    del a_ref
    i = pl.program_id(2)
    start = jnp.clip(i * QROWS - NA_KH // 2, 0, GH - NA_WIN_ROWS) * GW
    start = pl.multiple_of(start, 256)
    q = q_ref[...]
    kw = k_ref[pl.ds(start, NA_WIN), :]
    vw = v_ref[pl.ds(start, NA_WIN), :]
    kc = kc_ref[...]
    vc = vc_ref[...]
    lo = _lane_lo()
    outs = []
    for hh in range(2):
        own = lo if hh == 0 else jnp.logical_not(lo)
        qh = jnp.where(own, q, jnp.zeros_like(q))
        s_w = _dot_t(qh, kw) + bias_ref[hh].astype(F32)
        s_c = _dot_t(qh, kc)
        m = jnp.maximum(jnp.max(s_w, axis=1, keepdims=True), jnp.max(s_c, axis=1, keepdims=True))
        p_w = jnp.exp2((s_w - m).astype(BF16))
        p_c = jnp.exp2((s_c - m).astype(BF16))
        o = _dot(p_w, jnp.where(own, vw, jnp.ones_like(vw))) + _dot(p_c, jnp.where(own, vc, jnp.ones_like(vc)))
        outs.append(o / pltpu.roll(o, HD, 1))
    o_ref[...] = jnp.where(lo, outs[0], outs[1]).astype(BF16)


def _na(qkv, bias, j, a):
    nq = N // TQ

    def cls(i):
        return jnp.where(i == 0, 0, jnp.where(i == nq - 1, 2, 1))

    return pl.pallas_call(
        _na_kernel,
        grid=(4, B, nq),
        in_specs=[
            pl.BlockSpec((TQ, 128), lambda p, b, i: (b * nq + i, p)),
            pl.BlockSpec((N, 128), lambda p, b, i: (b, 4 + p)),
            pl.BlockSpec((N, 128), lambda p, b, i: (b, 8 + p)),
            pl.BlockSpec((CTX, 128), lambda p, b, i: (LAT // CTX + b, 4 + p)),
            pl.BlockSpec((CTX, 128), lambda p, b, i: (LAT // CTX + b, 8 + p)),
            pl.BlockSpec((None, None, None, 2, TQ, NA_WIN), lambda p, b, i: (j, p, cls(i), 0, 0, 0)),
            pl.BlockSpec(memory_space=pl.ANY),
        ],
        out_specs=pl.BlockSpec((TQ, 128), lambda p, b, i: (b * nq + i, 4 + p)),
        out_shape=jax.ShapeDtypeStruct((ROWS, D), BF16),
        input_output_aliases={6: 0},
        compiler_params=_cparams(("arbitrary", "arbitrary", "arbitrary")),
        name="na",
    )(qkv, qkv, qkv, qkv, qkv, bias, a)


def _na_ctx_kernel(q_ref, kc_ref, vc_ref, a_ref, o_ref):
    del a_ref
    q = q_ref[...]
    kc = kc_ref[...]
    vc = vc_ref[...]
    lo = _lane_lo()
    outs = []
    for hh in range(2):
        own = lo if hh == 0 else jnp.logical_not(lo)
        qh = jnp.where(own, q, jnp.zeros_like(q))
        o = _softmax_pv(_dot_t(qh, kc), jnp.where(own, vc, jnp.ones_like(vc)))
        outs.append(o / pltpu.roll(o, HD, 1))
    o_ref[...] = jnp.where(lo, outs[0], outs[1]).astype(BF16)


def _na_ctx(qkv, a):
    r0 = LAT // CTX
    return pl.pallas_call(
        _na_ctx_kernel,
        grid=(B, 4),
        in_specs=[
            pl.BlockSpec((CTX, 128), lambda b, p: (r0 + b, p)),
            pl.BlockSpec((CTX, 128), lambda b, p: (r0 + b, 4 + p)),
            pl.BlockSpec((CTX, 128), lambda b, p: (r0 + b, 8 + p)),
            pl.BlockSpec(memory_space=pl.ANY),
        ],
        out_specs=pl.BlockSpec((CTX, 128), lambda b, p: (r0 + b, 4 + p)),
        out_shape=jax.ShapeDtypeStruct((ROWS, D), BF16),
        input_output_aliases={3: 0},
        compiler_params=_cparams(("arbitrary", "arbitrary")),
        name="na_ctx",
    )(qkv, qkv, qkv, a)


def _diff_lambda(lq1_ref, lk1_ref, lq2_ref, lk2_ref, lam_init):
    s1 = jnp.sum(lq1_ref[...] * lk1_ref[...], axis=1, keepdims=True)
    s2 = jnp.sum(lq2_ref[...] * lk2_ref[...], axis=1, keepdims=True)
    return jnp.exp(s1) - jnp.exp(s2) + lam_init


def _diff_finish(o1, o2, lam, g, lam_init):
    o = o1 - lam * o2
    ms = jnp.mean(o * o, axis=-1, keepdims=True)
    return (o * lax.rsqrt(ms + SUBLN_EPS) * g * (1.0 - lam_init)).astype(BF16)


def _diff_kernel(lam_init, q_ref, kl_ref, vl_ref, kc_ref, vc_ref, lq1_ref, lk1_ref, lq2_ref, lk2_ref,
                 g_ref, o_ref, k_all, vt_ext, s_buf, m_buf):
    k_all[0:CTX, :] = kc_ref[...]
    k_all[CTX:, :] = kl_ref[...]
    vt_ext[0:128, 0:CTX] = vc_ref[...].astype(F32).T.astype(BF16)
    vt_ext[0:128, CTX:] = vl_ref[...].astype(F32).T.astype(BF16)
    vt_ext[128:, :] = jnp.ones((DIFF_SUM_ROWS, CTX + N), BF16)

    lam = _diff_lambda(lq1_ref, lk1_ref, lq2_ref, lk2_ref, lam_init)
    g_out = g_ref[...] * (1.0 - lam_init)
    lo = _lane_lo()
    nq = N // TQ
    nck = (CTX + N) // DIFF_CK

    def stage(qk_tile, qk_slot, pv_slot):
        if qk_slot is not None:
            q = q_ref[pl.ds(pl.multiple_of(qk_tile * TQ, TQ), TQ), :]
            qj = jnp.where(lo if qk_slot == 0 else jnp.logical_not(lo), q, jnp.zeros_like(q))
            m8 = None
        if pv_slot is not None:
            m = jnp.max(m_buf[pv_slot], axis=0, keepdims=True)
            acc = None
        for c in range(nck):
            rows = slice(c * DIFF_CK, (c + 1) * DIFF_CK)
            if qk_slot is not None:
                s = _dot_t(k_all[rows, :], qj)
                s_buf[qk_slot, rows, :] = s
                cm = jnp.max(s.reshape(DIFF_CK // 8, 8, TQ), axis=0)
                m8 = cm if m8 is None else jnp.maximum(m8, cm)
            if pv_slot is not None:
                p = jnp.exp2((s_buf[pv_slot, rows, :] - m).astype(BF16))
                d = _dot(vt_ext[:, rows], p)
                acc = d if acc is None else acc + d
        if qk_slot is not None:
            m_buf[qk_slot] = m8
        if pv_slot is not None:
            return acc[0:128, :] / acc[128:129, :]
        return None

    stage(0, 0, None)

    def tile_body(t, carry):
        o1 = stage(t, 1, 0)
        o2 = stage(jnp.minimum(t + 1, nq - 1), 0, 1)
        o = o1 - lam * o2
        ms = jnp.mean(o * o, axis=0, keepdims=True)
        on = (o * lax.rsqrt(ms + SUBLN_EPS)).T
        o_ref[pl.ds(pl.multiple_of(t * TQ, TQ), TQ), :] = (on * g_out).astype(BF16)
        return carry

    lax.fori_loop(0, nq, tile_body, 0)


def _diff_ctx_kernel(lam_init, q_ref, kc_ref, vc_ref, lq1_ref, lk1_ref, lq2_ref, lk2_ref, g_ref,
                     a_ref, o_ref):
    del a_ref
    lam = _diff_lambda(lq1_ref, lk1_ref, lq2_ref, lk2_ref, lam_init)
    q = q_ref[...]
    lo = _lane_lo()
    outs = []
    vc = vc_ref[...]
    v_ext = jnp.concatenate([vc, jnp.ones_like(vc)], axis=1)
    for jj in range(2):
        qj = jnp.where(lo if jj == 0 else jnp.logical_not(lo), q, jnp.zeros_like(q))
        o = _softmax_pv(_dot_t(qj, kc_ref[...]), v_ext)
        outs.append(o[:, :128] / o[:, 128:])
    o_ref[...] = _diff_finish(outs[0], outs[1], lam, g_ref[...], lam_init)


def _small_specs(n, nargs):
    return [pl.BlockSpec((1, n), (lambda *_: (0, 0))) for _ in range(nargs)]


def _diff(qkv, lams, g, lam_init):
    nq = N // TQ
    nh = D // 128
    r0 = LAT // CTX
    return pl.pallas_call(
        functools.partial(_diff_kernel, lam_init),
        grid=(B, nh),
        in_specs=[
            pl.BlockSpec((N, 128), lambda b, h: (b, h)),
            pl.BlockSpec((N, 128), lambda b, h: (b, nh + h)),
            pl.BlockSpec((N, 128), lambda b, h: (b, 2 * nh + h)),
            pl.BlockSpec((CTX, 128), lambda b, h: (r0 + b, nh + h)),
            pl.BlockSpec((CTX, 128), lambda b, h: (r0 + b, 2 * nh + h)),
        ] + _small_specs(HD, 4) + _small_specs(2 * HD, 1),
        out_specs=pl.BlockSpec((N, 128), lambda b, h: (b, h)),
        out_shape=jax.ShapeDtypeStruct((ROWS, D), BF16),
        scratch_shapes=[
            pltpu.VMEM((CTX + N, 128), BF16),
            pltpu.VMEM((128 + DIFF_SUM_ROWS, CTX + N), BF16),
            pltpu.VMEM((2, CTX + N, TQ), F32),
            pltpu.VMEM((2, 8, TQ), F32),
        ],
        compiler_params=_cparams(("arbitrary", "arbitrary")),
        name="diff_attn",
    )(qkv, qkv, qkv, qkv, qkv, *lams, g)


def _diff_ctx(qkv, lams, g, lam_init, a):
    nh = D // 128
    r0 = LAT // CTX
    return pl.pallas_call(
        functools.partial(_diff_ctx_kernel, lam_init),
        grid=(B, nh),
        in_specs=[
            pl.BlockSpec((CTX, 128), lambda b, h: (r0 + b, h)),
            pl.BlockSpec((CTX, 128), lambda b, h: (r0 + b, nh + h)),
            pl.BlockSpec((CTX, 128), lambda b, h: (r0 + b, 2 * nh + h)),
        ] + _small_specs(HD, 4) + _small_specs(2 * HD, 1) + [pl.BlockSpec(memory_space=pl.ANY)],
        out_specs=pl.BlockSpec((CTX, 128), lambda b, h: (r0 + b, h)),
        out_shape=jax.ShapeDtypeStruct((ROWS, D), BF16),
        input_output_aliases={8: 0},
        compiler_params=_cparams(("arbitrary", "arbitrary")),
        name="diff_attn_ctx",
    )(qkv, qkv, qkv, *lams, g, a)


def _post_kernel(last, x_ref, a_ref, mod_ref, g_ref, wo_ref, wi_ref, w2_ref, fg_ref, o_ref):
    x = x_ref[...]
    x2 = x + mod_ref[2:3, :] * _dot(a_ref[...], wo_ref[...])
    h = _norm_mod(x2, g_ref[...], mod_ref[3:4, :], mod_ref[4:5, :]).astype(BF16)
    acc = jnp.zeros((TM, D), F32)
    for c in range(MLP_H // D):
        u = jnp.maximum(_dot(h, wi_ref[:, c * D:(c + 1) * D]), 0.0)
        acc = acc + _dot((u * u).astype(BF16), w2_ref[c * D:(c + 1) * D, :])
    x3 = x2 + mod_ref[5:6, :] * acc
    if last:
        ms = jnp.mean(x3 * x3, axis=-1, keepdims=True)
        x3 = x3 * lax.rsqrt(ms + NORM_EPS) * fg_ref[...]
    o_ref[...] = x3


def _post(xs, a, mods_l, g, wo, wi, w2, fg, last):
    nt = NT_LAT if last else NT
    const = functools.partial(pl.BlockSpec, pipeline_mode=pl.Buffered(1))
    return pl.pallas_call(
        functools.partial(_post_kernel, last),
        grid=(nt,),
        in_specs=[
            pl.BlockSpec((TM, D), lambda t: (t, 0)),
            pl.BlockSpec((TM, D), lambda t: (t, 0)),
            pl.BlockSpec((None, 6, D), lambda t: (t // TILES_PER_BATCH, 0, 0)),
            pl.BlockSpec((1, D), lambda t: (0, 0)),
            const((D, D), lambda t: (0, 0)),
            const((D, MLP_H), lambda t: (0, 0)),
            const((MLP_H, D), lambda t: (0, 0)),
            pl.BlockSpec((1, D), lambda t: (0, 0)),
        ],
        out_specs=pl.BlockSpec((TM, D), lambda t: (t, 0)),
        out_shape=jax.ShapeDtypeStruct((nt * TM, D), F32),
        compiler_params=_cparams(("arbitrary",)),
        name="post_last" if last else "post",
    )(xs, a, mods_l, g, wo, wi, w2, fg)


def _channel_dft():
    n = np.arange(FD)
    ang = 2.0 * np.pi * ((n[:, None] * n[None, :]) % FD) / FD
    c = np.cos(ang) / math.sqrt(FD)
    s = np.sin(ang) / math.sqrt(FD)
    m = np.zeros((FW, 2 * FW), np.float32)
    for g in range(FG):
        m[g * FD:(g + 1) * FD, g * FD:(g + 1) * FD] = c
        m[g * FD:(g + 1) * FD, FW + g * FD:FW + (g + 1) * FD] = s
    return jnp.asarray(m).astype(BF16)


def _ctx_dft():
    n = np.arange(CTX)
    ang = 2.0 * np.pi * ((n[:, None] * n[None, :]) % CTX) / CTX
    m = np.concatenate([np.cos(ang), np.sin(ang)], axis=1) / math.sqrt(CTX)
    return jnp.asarray(m.astype(np.float32)).astype(BF16)


def _position_dft_tables():
    l = jnp.arange(N, dtype=jnp.int32)[None, :]
    dk = jnp.arange(TMF, dtype=jnp.int32)[:, None]
    ang = ((dk * l) % N).astype(F32) * (2.0 * math.pi / N)
    k0 = (jnp.arange(N // TMF, dtype=jnp.int32) * TMF)[:, None]
    ang0 = ((k0 * l) % N).astype(F32) * (2.0 * math.pi / N)
    scale = 1.0 / math.sqrt(N)
    ca = (jnp.cos(ang0) * scale).reshape(N // TMF, 1, N)
    sa = (jnp.sin(ang0) * scale).reshape(N // TMF, 1, N)
    return jnp.cos(ang), jnp.sin(ang), ca, sa


def _rope_tables():
    t = jnp.arange(N)
    rows, cols = t // GW, t % GW
    nf = HD // 4
    inv_freq = ROPE_THETA ** (-jnp.arange(nf, dtype=F32) / nf)
    lane = np.arange(128)
    d = lane % HD
    use_col = d >= HD // 2
    first = (d % (HD // 2)) < nf
    pos = jnp.where(use_col[None, :], cols[:, None], rows[:, None]).astype(F32)
    ang = pos * inv_freq[d % nf][None, :]
    cos, sin = jnp.cos(ang), jnp.sin(ang)
    sa = jnp.where(first[None, :], -sin, 0.0)
    sb = jnp.where(first[None, :], 0.0, sin)
    pad = jnp.zeros((TM, 128), F32)
    return (jnp.concatenate([cos, pad + 1.0]), jnp.concatenate([sa, pad]), jnp.concatenate([sb, pad]))


def kernel(x, c, ctx, c_ctx, w_mod, b_mod, norm_mix_g, norm_mlp_g, w_mlp_in, w_mlp_out, w_in_ab, w_out_ab,
           na_rpb, w_qkv_diff, w_out_diff, diff_lq1, diff_lk1, diff_lq2, diff_lk2, diff_subln_g,
           final_norm_g):
    assert x.shape == (B, N, D) and ctx.shape == (B, CTX, D)
    xs = jnp.concatenate([x.reshape(LAT, D), ctx.reshape(B * CTX, D)], axis=0)
    cin = jnp.concatenate([c, c_ctx[None, :], jnp.zeros((8 - B - 1, D), F32)], axis=0)
    mods = _modulation(cin, w_mod, b_mod).reshape(DEPTH, 8, 6, D)

    cs = _channel_dft()
    mctx = _ctx_dft()
    tc, ts, ca, sa = _position_dft_tables()
    rope = _rope_tables()
    bias = _na_bias(na_rpb.reshape(-1))
    fg = final_norm_g.reshape(1, D)

    for i in range(DEPTH):
        last = i == DEPTH - 1
        j = i // 2
        g_mix = norm_mix_g[i].reshape(1, D)
        g_mlp = norm_mlp_g[i].reshape(1, D)
        if i % 2 == 0:
            fcs, qkv = _proj_even(xs, mods[i], g_mix, w_in_ab[j].astype(BF16), cs)
            a = _fourier(fcs, tc, ts, ca, sa)
            a = _na(qkv, bias, j, a)
            if not last:
                a = _fourier_ctx(fcs, mctx, a)
                a = _na_ctx(qkv, a)
            wo = w_out_ab[j].astype(BF16)
        else:
            lam_init = 0.8 - 0.6 * math.exp(-0.3 * i)
            qkv = _proj_odd(xs, mods[i], g_mix, w_qkv_diff[j].astype(BF16), *rope)
            lams = [v[j].reshape(1, HD) for v in (diff_lq1, diff_lk1, diff_lq2, diff_lk2)]
            g_sub = diff_subln_g[j].reshape(1, 2 * HD)
            a = _diff(qkv, lams, g_sub, lam_init)
            if not last:
                a = _diff_ctx(qkv, lams, g_sub, lam_init, a)
            wo = w_out_diff[j].astype(BF16)
        xs = _post(xs, a, mods[i], g_mlp, wo, w_mlp_in[i].astype(BF16), w_mlp_out[i].astype(BF16), fg, last)
    return xs.reshape(B, N, D)
```

```python
import functools
import math

import numpy as np
import jax
import jax.numpy as jnp
from jax import lax
from jax.experimental import pallas as pl
from jax.experimental.pallas import tpu as pltpu

F32 = jnp.float32
BF16 = jnp.bfloat16

D = 1024
B = 4
N = 4096
CTX = 256
DEPTH = 4
GW = 64
GH = N // GW
FW = 512
FG = 4
FD = 128
NAW = 512
HD = 64
NA_KH = 8
NA_KW = 16
MLP_H = 4 * D
NORM_EPS = 1e-6
SUBLN_EPS = 1e-5
NEG = -1e30
ROPE_THETA = 10000.0
LOG2E = 1.4426950408889634
Q_SCALE = HD ** -0.5 * LOG2E

TM = 512
LAT = B * N
ROWS = LAT + B * CTX
NT = ROWS // TM
NT_LAT = LAT // TM
TILES_PER_BATCH = N // TM
TQ = 512
QROWS = TQ // GW
NA_WIN_ROWS = 16
NA_WIN = NA_WIN_ROWS * GW
NA_CK = 256
TMF = 256
DIFF_SUM_ROWS = 16
DIFF_CHUNKS = tuple(range(0, CTX + N + 1, 256))

VMEM_LIMIT = 60 * 1024 * 1024


def _cparams(sem):
    return pltpu.CompilerParams(dimension_semantics=sem, vmem_limit_bytes=VMEM_LIMIT)


def _mod_kernel(cin_ref, w_ref, b_ref, o_ref):
    cin = cin_ref[...]
    act = cin * jax.nn.sigmoid(cin)
    o_ref[...] = jnp.dot(act, w_ref[...], preferred_element_type=F32,
                         precision=lax.Precision.HIGHEST) + b_ref[...]


def _modulation(cin, w_mod, b_mod):
    tn = 1536
    return pl.pallas_call(
        _mod_kernel,
        grid=(DEPTH, 6 * D // tn),
        in_specs=[
            pl.BlockSpec((8, D), lambda l, j: (0, 0)),
            pl.BlockSpec((None, D, tn), lambda l, j: (l, 0, j)),
            pl.BlockSpec((None, 1, tn), lambda l, j: (l, 0, j)),
        ],
        out_specs=pl.BlockSpec((None, 8, tn), lambda l, j: (l, 0, j)),
        out_shape=jax.ShapeDtypeStruct((DEPTH, 8, 6 * D), F32),
        compiler_params=_cparams(("arbitrary", "arbitrary")),
        name="modulation",
    )(cin, w_mod, b_mod.reshape(DEPTH, 1, 6 * D))


def _norm_mod(x, g, shift, scale):
    ms = jnp.mean(x * x, axis=-1, keepdims=True)
    y = x * lax.rsqrt(ms + NORM_EPS) * g
    return y * (1.0 + scale) + shift


def _dot(a, b):
    return jnp.dot(a, b, preferred_element_type=F32)


def _dot_t(a, b):
    return lax.dot_general(a, b, (((1,), (1,)), ((), ())), preferred_element_type=F32)


def _softmax_pv(s, v_ext):
    m = jnp.max(s, axis=1, keepdims=True)
    p = jnp.exp2((s - m).astype(BF16))
    return _dot(p, v_ext)


def _lane_lo():
    return lax.broadcasted_iota(jnp.int32, (1, 2 * HD), 1) < HD


def _proj_even_kernel(x_ref, mod_ref, g_ref, w_ref, cs_ref, fcs_ref, qkv_ref):
    h = _norm_mod(x_ref[...], g_ref[...], mod_ref[0:1, :], mod_ref[1:2, :]).astype(BF16)
    p = _dot(h, w_ref[...])
    f = p[:, :FW].astype(BF16)
    fcs_ref[...] = _dot(f, cs_ref[...]).astype(BF16)
    qkv_ref[:, :NAW] = (p[:, FW:FW + NAW] * Q_SCALE).astype(BF16)
    qkv_ref[:, NAW:] = p[:, FW + NAW:].astype(BF16)


def _proj_even(xs, mods_l, g, w, cs):
    nout = FW + 3 * NAW
    return pl.pallas_call(
        _proj_even_kernel,
        grid=(NT,),
        in_specs=[
            pl.BlockSpec((TM, D), lambda t: (t, 0)),
            pl.BlockSpec((None, 6, D), lambda t: (t // TILES_PER_BATCH, 0, 0)),
            pl.BlockSpec((1, D), lambda t: (0, 0)),
            pl.BlockSpec((D, nout), lambda t: (0, 0)),
            pl.BlockSpec((FW, 2 * FW), lambda t: (0, 0)),
        ],
        out_specs=[
            pl.BlockSpec((TM, 2 * FW), lambda t: (t, 0)),
            pl.BlockSpec((TM, 3 * NAW), lambda t: (t, 0)),
        ],
        out_shape=[
            jax.ShapeDtypeStruct((ROWS, 2 * FW), BF16),
            jax.ShapeDtypeStruct((ROWS, 3 * NAW), BF16),
        ],
        compiler_params=_cparams(("arbitrary",)),
        name="proj_even",
    )(xs, mods_l, g, w, cs)


def _proj_odd_kernel(x_ref, mod_ref, g_ref, w_ref, cos_ref, sa_ref, sb_ref, qkv_ref):
    h = _norm_mod(x_ref[...], g_ref[...], mod_ref[0:1, :], mod_ref[1:2, :]).astype(BF16)
    p = _dot(h, w_ref[...])
    cos = cos_ref[...]
    sa = sa_ref[...]
    sb = sb_ref[...]
    for c in range(2 * D // 128):
        xc = p[:, c * 128:(c + 1) * 128]
        r = xc * cos + pltpu.roll(xc, 112, 1) * sa + pltpu.roll(xc, 16, 1) * sb
        if c < D // 128:
            r = r * Q_SCALE
        qkv_ref[:, c * 128:(c + 1) * 128] = r.astype(BF16)
    qkv_ref[:, 2 * D:] = p[:, 2 * D:].astype(BF16)


def _proj_odd(xs, mods_l, g, w, cos, sa, sb):
    nout = 3 * D

    def rope_idx(t):
        return (jnp.where(t < NT_LAT, t % TILES_PER_BATCH, TILES_PER_BATCH), 0)

    return pl.pallas_call(
        _proj_odd_kernel,
        grid=(NT,),
        in_specs=[
            pl.BlockSpec((TM, D), lambda t: (t, 0)),
            pl.BlockSpec((None, 6, D), lambda t: (t // TILES_PER_BATCH, 0, 0)),
            pl.BlockSpec((1, D), lambda t: (0, 0)),
            pl.BlockSpec((D, nout), lambda t: (0, 0)),
            pl.BlockSpec((TM, 128), rope_idx),
            pl.BlockSpec((TM, 128), rope_idx),
            pl.BlockSpec((TM, 128), rope_idx),
        ],
        out_specs=pl.BlockSpec((TM, nout), lambda t: (t, 0)),
        out_shape=jax.ShapeDtypeStruct((ROWS, nout), BF16),
        compiler_params=_cparams(("arbitrary",)),
        name="proj_odd",
    )(xs, mods_l, g, w, cos, sa, sb)


def _fourier_kernel(x_ref, tc_ref, ts_ref, ca_ref, sa_ref, o_ref):
    tc = tc_ref[...]
    ts = ts_ref[...]
    ca = ca_ref[...]
    sa = sa_ref[...]
    mc = (tc * ca - ts * sa).astype(BF16)
    ms = (ts * ca + tc * sa).astype(BF16)
    y = _dot(mc, x_ref[:, :FW]) - _dot(ms, x_ref[:, FW:])
    o_ref[...] = y.astype(BF16)


def _fourier(fcs, tc, ts, ca, sa):
    nti = N // TMF
    return pl.pallas_call(
        _fourier_kernel,
        grid=(B, nti),
        in_specs=[
            pl.BlockSpec((N, 2 * FW), lambda b, i: (b, 0)),
            pl.BlockSpec((TMF, N), lambda b, i: (0, 0)),
            pl.BlockSpec((TMF, N), lambda b, i: (0, 0)),
            pl.BlockSpec((None, 1, N), lambda b, i: (i, 0, 0)),
            pl.BlockSpec((None, 1, N), lambda b, i: (i, 0, 0)),
        ],
        out_specs=pl.BlockSpec((TMF, FW), lambda b, i: (b * nti + i, 0)),
        out_shape=jax.ShapeDtypeStruct((LAT, FW), BF16),
        compiler_params=_cparams(("arbitrary", "arbitrary")),
        name="fourier",
    )(fcs, tc, ts, ca, sa)


def _fourier_ctx_kernel(x_ref, m_ref, o_ref):
    y = _dot(m_ref[:, :CTX], x_ref[:, :FW]) - _dot(m_ref[:, CTX:], x_ref[:, FW:])
    o_ref[...] = y.astype(BF16)


def _fourier_ctx(fcs, mctx):
    return pl.pallas_call(
        _fourier_ctx_kernel,
        grid=(B,),
        in_specs=[
            pl.BlockSpec((CTX, 2 * FW), lambda b: (LAT // CTX + b, 0)),
            pl.BlockSpec((CTX, 2 * CTX), lambda b: (0, 0)),
        ],
        out_specs=pl.BlockSpec((CTX, FW), lambda b: (b, 0)),
        out_shape=jax.ShapeDtypeStruct((B * CTX, FW), BF16),
        compiler_params=_cparams(("arbitrary",)),
        name="fourier_ctx",
    )(fcs, mctx)


def _na_bias_kernel(rpb_ref, o_ref):
    j = pl.program_id(0)
    p = pl.program_id(1)
    cq = lax.broadcasted_iota(jnp.int32, (GW, 2 * GW), 0)
    lane = lax.broadcasted_iota(jnp.int32, (GW, 2 * GW), 1)
    ck = lane & (GW - 1)
    left = lane < GW
    cs = jnp.clip(cq - NA_KW // 2, 0, GW - NA_KW)
    col_ok = (ck >= cs) & (ck < cs + NA_KW)
    bidx = ck - cq + (NA_KW - 1)
    zero = jnp.zeros((GW, 2 * GW), F32)
    neg = jnp.full((GW, 2 * GW), NEG, F32)
    neg_l = jnp.where(left, neg, zero)
    neg_r = jnp.where(left, zero, neg)
    nb = 2 * NA_KW - 1
    na = 2 * NA_KH - 1
    for hh in range(2):
        base = ((j * 8 + 2 * p + hh) * na) * nb
        halves_l, halves_r = [], []
        for a in range(na):
            r = zero
            for b in range(nb):
                r = jnp.where(bidx == b, rpb_ref[base + a * nb + b], r)
            r = jnp.where(col_ok, r * LOG2E, neg)
            halves_l.append(jnp.where(left, r, zero))
            halves_r.append(jnp.where(left, zero, r))
        for cls, r0 in enumerate((0, QROWS, GH - QROWS)):
            start = min(max(r0 - NA_KH // 2, 0), GH - NA_WIN_ROWS)
            for rql in range(QROWS):
                rq = r0 + rql
                rs = min(max(rq - NA_KH // 2, 0), GH - NA_KH)
                for m in range(NA_WIN_ROWS // 2):
                    rk0 = start + 2 * m
                    rk1 = rk0 + 1
                    tl = halves_l[rk0 - rq + NA_KH - 1] if rs <= rk0 < rs + NA_KH else neg_l
                    tr = halves_r[rk1 - rq + NA_KH - 1] if rs <= rk1 < rs + NA_KH else neg_r
                    o_ref[cls, hh, rql * GW:(rql + 1) * GW, m * 128:(m + 1) * 128] = (tl + tr).astype(BF16)


def _na_bias(rpb_flat):
    n_even = (DEPTH + 1) // 2
    return pl.pallas_call(
        _na_bias_kernel,
        grid=(n_even, 4),
        in_specs=[pl.BlockSpec(memory_space=pltpu.SMEM)],
        out_specs=pl.BlockSpec((None, None, 3, 2, TQ, NA_WIN), lambda j, p: (j, p, 0, 0, 0, 0)),
        out_shape=jax.ShapeDtypeStruct((n_even, 4, 3, 2, TQ, NA_WIN), BF16),
        compiler_params=_cparams(("arbitrary", "arbitrary")),
        name="na_bias",
    )(rpb_flat)


def _na_kernel(q_ref, k_ref, v_ref, kc_ref, vc_ref, bias_ref, o_ref, s_buf, m_buf):
    nq = N // TQ
    nwc = NA_WIN // NA_CK
    lo = _lane_lo()
    kc = kc_ref[...]
    vc = vc_ref[...]

    def win_start(t):
        return pl.multiple_of(jnp.clip(t * QROWS - NA_KH // 2, 0, GH - NA_WIN_ROWS) * GW, 256)

    def stage(qk_t, qk_h, pv_t, pv_h):
        if qk_h is not None:
            q = q_ref[pl.ds(pl.multiple_of(qk_t * TQ, TQ), TQ), :]
            qh = jnp.where(lo if qk_h == 0 else jnp.logical_not(lo), q, jnp.zeros_like(q))
            k0 = win_start(qk_t)
            cls = jnp.where(qk_t == 0, 0, jnp.where(qk_t == nq - 1, 2, 1))
            m128 = None
        if pv_h is not None:
            m = jnp.max(m_buf[pv_h], axis=1, keepdims=True)
            v0 = win_start(pv_t)
            own = lo if pv_h == 0 else jnp.logical_not(lo)
            acc = None
        for c in range(nwc + 1):
            cols = slice(c * NA_CK, (c + 1) * NA_CK)
            if qk_h is not None:
                if c < nwc:
                    s = _dot_t(qh, k_ref[pl.ds(k0 + c * NA_CK, NA_CK), :])
                    s = s + bias_ref[cls, qk_h, :, cols].astype(F32)
                else:
                    s = _dot_t(qh, kc)
                s_buf[qk_h, :, cols] = s
                cm = jnp.maximum(s[:, :128], s[:, 128:])
                m128 = cm if m128 is None else jnp.maximum(m128, cm)
            if pv_h is not None:
                p = jnp.exp2((s_buf[pv_h, :, cols] - m).astype(BF16))
                vb = v_ref[pl.ds(v0 + c * NA_CK, NA_CK), :] if c < nwc else vc
                d = _dot(p, jnp.where(own, vb, jnp.ones_like(vb)))
                acc = d if acc is None else acc + d
        if qk_h is not None:
            m_buf[qk_h] = m128
        if pv_h is not None:
            return acc / pltpu.roll(acc, HD, 1)
        return None

    stage(0, 0, None, None)

    def tile_body(t, carry):
        o0 = stage(t, 1, t, 0)
        o1 = stage(jnp.minimum(t + 1, nq - 1), 0, t, 1)
        o_ref[pl.ds(pl.multiple_of(t * TQ, TQ), TQ), :] = jnp.where(lo, o0, o1).astype(BF16)
        return carry

    lax.fori_loop(0, nq, tile_body, 0)


def _na(qkv, bias, j):
    return pl.pallas_call(
        _na_kernel,
        grid=(4, B),
        in_specs=[
            pl.BlockSpec((N, 128), lambda p, b: (b, p)),
            pl.BlockSpec((N, 128), lambda p, b: (b, 4 + p)),
            pl.BlockSpec((N, 128), lambda p, b: (b, 8 + p)),
            pl.BlockSpec((CTX, 128), lambda p, b: (LAT // CTX + b, 4 + p)),
            pl.BlockSpec((CTX, 128), lambda p, b: (LAT // CTX + b, 8 + p)),
            pl.BlockSpec((None, None, 3, 2, TQ, NA_WIN), lambda p, b: (j, p, 0, 0, 0, 0)),
        ],
        out_specs=pl.BlockSpec((N, 128), lambda p, b: (b, p)),
        out_shape=jax.ShapeDtypeStruct((LAT, NAW), BF16),
        scratch_shapes=[
            pltpu.VMEM((2, TQ, NA_WIN + CTX), F32),
            pltpu.VMEM((2, TQ, 128), F32),
        ],
        compiler_params=_cparams(("arbitrary", "arbitrary")),
        name="na",
    )(qkv, qkv, qkv, qkv, qkv, bias)


def _na_ctx_kernel(q_ref, kc_ref, vc_ref, o_ref):
    q = q_ref[...]
    kc = kc_ref[...]
    vc = vc_ref[...]
    lo = _lane_lo()
    outs = []
    for hh in range(2):
        own = lo if hh == 0 else jnp.logical_not(lo)
        qh = jnp.where(own, q, jnp.zeros_like(q))
        o = _softmax_pv(_dot_t(qh, kc), jnp.where(own, vc, jnp.ones_like(vc)))
        outs.append(o / pltpu.roll(o, HD, 1))
    o_ref[...] = jnp.where(lo, outs[0], outs[1]).astype(BF16)


def _na_ctx(qkv):
    r0 = LAT // CTX
    return pl.pallas_call(
        _na_ctx_kernel,
        grid=(B, 4),
        in_specs=[
            pl.BlockSpec((CTX, 128), lambda b, p: (r0 + b, p)),
            pl.BlockSpec((CTX, 128), lambda b, p: (r0 + b, 4 + p)),
            pl.BlockSpec((CTX, 128), lambda b, p: (r0 + b, 8 + p)),
        ],
        out_specs=pl.BlockSpec((CTX, 128), lambda b, p: (b, p)),
        out_shape=jax.ShapeDtypeStruct((B * CTX, NAW), BF16),
        compiler_params=_cparams(("arbitrary", "arbitrary")),
        name="na_ctx",
    )(qkv, qkv, qkv)


def _diff_lambda(lq1_ref, lk1_ref, lq2_ref, lk2_ref, lam_init):
    s1 = jnp.sum(lq1_ref[...] * lk1_ref[...], axis=1, keepdims=True)
    s2 = jnp.sum(lq2_ref[...] * lk2_ref[...], axis=1, keepdims=True)
    return jnp.exp(s1) - jnp.exp(s2) + lam_init


def _diff_finish(o1, o2, lam, g, lam_init):
    o = o1 - lam * o2
    ms = jnp.mean(o * o, axis=-1, keepdims=True)
    return (o * lax.rsqrt(ms + SUBLN_EPS) * g * (1.0 - lam_init)).astype(BF16)


def _diff_kernel(lam_init, q_ref, kl_ref, vl_ref, kc_ref, vc_ref, lq1_ref, lk1_ref, lq2_ref, lk2_ref,
                 g_ref, o_ref, k_all, vt_ext, s_buf, m_buf):
    k_all[0:CTX, :] = kc_ref[...]
    k_all[CTX:, :] = kl_ref[...]
    vt_ext[0:128, 0:CTX] = vc_ref[...].astype(F32).T.astype(BF16)
    vt_ext[0:128, CTX:] = vl_ref[...].astype(F32).T.astype(BF16)
    vt_ext[128:, :] = jnp.ones((DIFF_SUM_ROWS, CTX + N), BF16)

    lam = _diff_lambda(lq1_ref, lk1_ref, lq2_ref, lk2_ref, lam_init)
    g_out = g_ref[...] * (1.0 - lam_init)
    lo = _lane_lo()
    nq = N // TQ

    def stage(qk_tile, qk_slot, pv_slot):
        if qk_slot is not None:
            q = q_ref[pl.ds(pl.multiple_of(qk_tile * TQ, TQ), TQ), :]
            qj = jnp.where(lo if qk_slot == 0 else jnp.logical_not(lo), q, jnp.zeros_like(q))
            m8 = None
        if pv_slot is not None:
            m = jnp.max(m_buf[pv_slot], axis=0, keepdims=True)
            acc = None

        def numerators(c, m_c):
            rows = slice(DIFF_CHUNKS[c], DIFF_CHUNKS[c + 1])
            return jnp.exp2((s_buf[pv_slot, rows, :] - m_c).astype(BF16))

        nck = len(DIFF_CHUNKS) - 1
        p_next = numerators(0, m) if pv_slot is not None else None
        cm_prev = None
        for c in range(nck):
            rows = slice(DIFF_CHUNKS[c], DIFF_CHUNKS[c + 1])
            m_c = m if pv_slot is not None else None
            if qk_slot is not None:
                if pv_slot is not None and cm_prev is not None:
                    m_c = jnp.maximum(m, jnp.minimum(cm_prev[0:1, :], m))
                s = _dot_t(k_all[rows, :], qj)
                s_buf[qk_slot, rows, :] = s
                cm_prev = jnp.max(s.reshape(-1, 8, TQ), axis=0)
                m8 = cm_prev if m8 is None else jnp.maximum(m8, cm_prev)
            if pv_slot is not None:
                p = p_next
                if c + 1 < nck:
                    p_next = numerators(c + 1, m_c)
                d = _dot(vt_ext[:, rows], p)
                acc = d if acc is None else acc + d
        if qk_slot is not None:
            m_buf[qk_slot] = m8
        if pv_slot is not None:
            return acc[0:128, :] / acc[128:129, :]
        return None

    stage(0, 0, None)

    def tile_body(t, carry):
        o1 = stage(t, 1, 0)
        o2 = stage(jnp.minimum(t + 1, nq - 1), 0, 1)
        o = o1 - lam * o2
        ms = jnp.mean(o * o, axis=0, keepdims=True)
        on = (o * lax.rsqrt(ms + SUBLN_EPS)).T
        o_ref[pl.ds(pl.multiple_of(t * TQ, TQ), TQ), :] = (on * g_out).astype(BF16)
        return carry

    lax.fori_loop(0, nq, tile_body, 0)


def _diff_ctx_kernel(lam_init, q_ref, kc_ref, vc_ref, lq1_ref, lk1_ref, lq2_ref, lk2_ref, g_ref, o_ref):
    lam = _diff_lambda(lq1_ref, lk1_ref, lq2_ref, lk2_ref, lam_init)
    q = q_ref[...]
    lo = _lane_lo()
    outs = []
    vc = vc_ref[...]
    v_ext = jnp.concatenate([vc, jnp.ones_like(vc)], axis=1)
    for jj in range(2):
        qj = jnp.where(lo if jj == 0 else jnp.logical_not(lo), q, jnp.zeros_like(q))
        o = _softmax_pv(_dot_t(qj, kc_ref[...]), v_ext)
        outs.append(o[:, :128] / o[:, 128:])
    o_ref[...] = _diff_finish(outs[0], outs[1], lam, g_ref[...], lam_init)


def _small_specs(n, nargs):
    return [pl.BlockSpec((1, n), (lambda *_: (0, 0))) for _ in range(nargs)]


def _diff(qkv, lams, g, lam_init):
    nq = N // TQ
    nh = D // 128
    r0 = LAT // CTX
    return pl.pallas_call(
        functools.partial(_diff_kernel, lam_init),
        grid=(B, nh),
        in_specs=[
            pl.BlockSpec((N, 128), lambda b, h: (b, h)),
            pl.BlockSpec((N, 128), lambda b, h: (b, nh + h)),
            pl.BlockSpec((N, 128), lambda b, h: (b, 2 * nh + h)),
            pl.BlockSpec((CTX, 128), lambda b, h: (r0 + b, nh + h)),
            pl.BlockSpec((CTX, 128), lambda b, h: (r0 + b, 2 * nh + h)),
        ] + _small_specs(HD, 4) + _small_specs(2 * HD, 1),
        out_specs=pl.BlockSpec((N, 128), lambda b, h: (b, h)),
        out_shape=jax.ShapeDtypeStruct((LAT, D), BF16),
        scratch_shapes=[
            pltpu.VMEM((CTX + N, 128), BF16),
            pltpu.VMEM((128 + DIFF_SUM_ROWS, CTX + N), BF16),
            pltpu.VMEM((2, CTX + N, TQ), F32),
            pltpu.VMEM((2, 8, TQ), F32),
        ],
        compiler_params=_cparams(("arbitrary", "arbitrary")),
        name="diff_attn",
    )(qkv, qkv, qkv, qkv, qkv, *lams, g)


def _diff_ctx(qkv, lams, g, lam_init):
    nh = D // 128
    r0 = LAT // CTX
    return pl.pallas_call(
        functools.partial(_diff_ctx_kernel, lam_init),
        grid=(B, nh),
        in_specs=[
            pl.BlockSpec((CTX, 128), lambda b, h: (r0 + b, h)),
            pl.BlockSpec((CTX, 128), lambda b, h: (r0 + b, nh + h)),
            pl.BlockSpec((CTX, 128), lambda b, h: (r0 + b, 2 * nh + h)),
        ] + _small_specs(HD, 4) + _small_specs(2 * HD, 1),
        out_specs=pl.BlockSpec((CTX, 128), lambda b, h: (b, h)),
        out_shape=jax.ShapeDtypeStruct((B * CTX, D), BF16),
        compiler_params=_cparams(("arbitrary", "arbitrary")),
        name="diff_attn_ctx",
    )(qkv, qkv, qkv, *lams, g)


def _post_kernel(last, widths, x_ref, *refs):
    n_att = len(widths) * (1 if last else 2)
    att_refs = refs[:n_att]
    mod_ref, g_ref, wo_ref, wi_ref, w2_ref, fg_ref, o_ref = refs[n_att:]
    x = x_ref[...]
    is_latent = pl.program_id(0) < NT_LAT
    y = None
    off = 0
    for i, w in enumerate(widths):
        if last:
            a = att_refs[i][...]
        else:
            a = jnp.where(is_latent, att_refs[2 * i][...], att_refs[2 * i + 1][...])
        d = _dot(a, wo_ref[off:off + w, :])
        y = d if y is None else y + d
        off += w
    x2 = x + mod_ref[2:3, :] * y
    h = _norm_mod(x2, g_ref[...], mod_ref[3:4, :], mod_ref[4:5, :]).astype(BF16)
    acc = jnp.zeros((TM, D), F32)
    for c in range(MLP_H // D):
        u = jnp.maximum(_dot(h, wi_ref[:, c * D:(c + 1) * D]), 0.0)
        acc = acc + _dot((u * u).astype(BF16), w2_ref[c * D:(c + 1) * D, :])
    x3 = x2 + mod_ref[5:6, :] * acc
    if last:
        ms = jnp.mean(x3 * x3, axis=-1, keepdims=True)
        x3 = x3 * lax.rsqrt(ms + NORM_EPS) * fg_ref[...]
    o_ref[...] = x3


def _post(xs, att, mods_l, g, wo, wi, w2, fg, last):
    nt = NT_LAT if last else NT
    const = functools.partial(pl.BlockSpec, pipeline_mode=pl.Buffered(1))
    widths = tuple(a_lat.shape[1] for a_lat, _ in att)
    assert sum(widths) == D
    att_specs, att_args = [], []
    for (a_lat, a_ctx), w in zip(att, widths):
        att_specs.append(pl.BlockSpec((TM, w), lambda t: (jnp.minimum(t, NT_LAT - 1), 0)))
        att_args.append(a_lat)
        if not last:
            att_specs.append(pl.BlockSpec((TM, w), lambda t: (jnp.maximum(t - NT_LAT, 0), 0)))
            att_args.append(a_ctx)
    return pl.pallas_call(
        functools.partial(_post_kernel, last, widths),
        grid=(nt,),
        in_specs=[pl.BlockSpec((TM, D), lambda t: (t, 0))] + att_specs + [
            pl.BlockSpec((None, 6, D), lambda t: (t // TILES_PER_BATCH, 0, 0)),
            pl.BlockSpec((1, D), lambda t: (0, 0)),
            const((D, D), lambda t: (0, 0)),
            const((D, MLP_H), lambda t: (0, 0)),
            const((MLP_H, D), lambda t: (0, 0)),
            pl.BlockSpec((1, D), lambda t: (0, 0)),
        ],
        out_specs=pl.BlockSpec((TM, D), lambda t: (t, 0)),
        out_shape=jax.ShapeDtypeStruct((nt * TM, D), F32),
        compiler_params=_cparams(("arbitrary",)),
        name="post_last" if last else "post",
    )(xs, *att_args, mods_l, g, wo, wi, w2, fg)


def _channel_dft():
    n = np.arange(FD)
    ang = 2.0 * np.pi * ((n[:, None] * n[None, :]) % FD) / FD
    c = np.cos(ang) / math.sqrt(FD)
    s = np.sin(ang) / math.sqrt(FD)
    m = np.zeros((FW, 2 * FW), np.float32)
    for g in range(FG):
        m[g * FD:(g + 1) * FD, g * FD:(g + 1) * FD] = c
        m[g * FD:(g + 1) * FD, FW + g * FD:FW + (g + 1) * FD] = s
    return jnp.asarray(m).astype(BF16)


def _ctx_dft():
    n = np.arange(CTX)
    ang = 2.0 * np.pi * ((n[:, None] * n[None, :]) % CTX) / CTX
    m = np.concatenate([np.cos(ang), np.sin(ang)], axis=1) / math.sqrt(CTX)
    return jnp.asarray(m.astype(np.float32)).astype(BF16)


def _position_dft_tables():
    l = jnp.arange(N, dtype=jnp.int32)[None, :]
    dk = jnp.arange(TMF, dtype=jnp.int32)[:, None]
    ang = ((dk * l) % N).astype(F32) * (2.0 * math.pi / N)
    k0 = (jnp.arange(N // TMF, dtype=jnp.int32) * TMF)[:, None]
    ang0 = ((k0 * l) % N).astype(F32) * (2.0 * math.pi / N)
    scale = 1.0 / math.sqrt(N)
    ca = (jnp.cos(ang0) * scale).reshape(N // TMF, 1, N)
    sa = (jnp.sin(ang0) * scale).reshape(N // TMF, 1, N)
    return jnp.cos(ang), jnp.sin(ang), ca, sa


def _rope_tables():
    t = jnp.arange(N)
    rows, cols = t // GW, t % GW
    nf = HD // 4
    inv_freq = ROPE_THETA ** (-jnp.arange(nf, dtype=F32) / nf)
    lane = np.arange(128)
    d = lane % HD
    use_col = d >= HD // 2
    first = (d % (HD // 2)) < nf
    pos = jnp.where(use_col[None, :], cols[:, None], rows[:, None]).astype(F32)
    ang = pos * inv_freq[d % nf][None, :]
    cos, sin = jnp.cos(ang), jnp.sin(ang)
    sa = jnp.where(first[None, :], -sin, 0.0)
    sb = jnp.where(first[None, :], 0.0, sin)
    pad = jnp.zeros((TM, 128), F32)
    return (jnp.concatenate([cos, pad + 1.0]), jnp.concatenate([sa, pad]), jnp.concatenate([sb, pad]))


def kernel(x, c, ctx, c_ctx, w_mod, b_mod, norm_mix_g, norm_mlp_g, w_mlp_in, w_mlp_out, w_in_ab, w_out_ab,
           na_rpb, w_qkv_diff, w_out_diff, diff_lq1, diff_lk1, diff_lq2, diff_lk2, diff_subln_g,
           final_norm_g):
    assert x.shape == (B, N, D) and ctx.shape == (B, CTX, D)
    xs = jnp.concatenate([x.reshape(LAT, D), ctx.reshape(B * CTX, D)], axis=0)
    cin = jnp.concatenate([c, c_ctx[None, :], jnp.zeros((8 - B - 1, D), F32)], axis=0)
    mods = _modulation(cin, w_mod, b_mod).reshape(DEPTH, 8, 6, D)

    cs = _channel_dft()
    mctx = _ctx_dft()
    tc, ts, ca, sa = _position_dft_tables()
    rope = _rope_tables()
    bias = _na_bias(na_rpb.reshape(-1))
    fg = final_norm_g.reshape(1, D)

    for i in range(DEPTH):
        last = i == DEPTH - 1
        j = i // 2
        g_mix = norm_mix_g[i].reshape(1, D)
        g_mlp = norm_mlp_g[i].reshape(1, D)
        if i % 2 == 0:
            fcs, qkv = _proj_even(xs, mods[i], g_mix, w_in_ab[j].astype(BF16), cs)
            att = [(_fourier(fcs, tc, ts, ca, sa), None if last else _fourier_ctx(fcs, mctx)),
                   (_na(qkv, bias, j), None if last else _na_ctx(qkv))]
            wo = w_out_ab[j].astype(BF16)
        else:
            lam_init = 0.8 - 0.6 * math.exp(-0.3 * i)
            qkv = _proj_odd(xs, mods[i], g_mix, w_qkv_diff[j].astype(BF16), *rope)
            lams = [v[j].reshape(1, HD) for v in (diff_lq1, diff_lk1, diff_lq2, diff_lk2)]
            g_sub = diff_subln_g[j].reshape(1, 2 * HD)
            att = [(_diff(qkv, lams, g_sub, lam_init), None if last else _diff_ctx(qkv, lams, g_sub, lam_init))]
            wo = w_out_diff[j].astype(BF16)
        xs = _post(xs, att, mods[i], g_mlp, wo, w_mlp_in[i].astype(BF16), w_mlp_out[i].astype(BF16), fg, last)
    return xs.reshape(B, N, D)
```

```python
import functools
import math

import numpy as np
import jax
import jax.numpy as jnp
from jax import lax
from jax.experimental import pallas as pl
from jax.experimental.pallas import tpu as pltpu

F32 = jnp.float32
BF16 = jnp.bfloat16

D = 1024
B = 4
N = 4096
CTX = 256
DEPTH = 4
GW = 64
GH = N // GW
FW = 512
FG = 4
FD = 128
NAW = 512
HD = 64
NA_KH = 8
NA_KW = 16
MLP_H = 4 * D
NORM_EPS = 1e-6
SUBLN_EPS = 1e-5
NEG = -1e30
ROPE_THETA = 10000.0
LOG2E = 1.4426950408889634
Q_SCALE = HD ** -0.5 * LOG2E

TM = 512
LAT = B * N
ROWS = LAT + B * CTX
NT = ROWS // TM
NT_LAT = LAT // TM
TILES_PER_BATCH = N // TM
TQ = 512
QROWS = TQ // GW
NA_WIN_ROWS = 16
NA_WIN = NA_WIN_ROWS * GW
NA_CK = 256
TMF = 256
DIFF_SUM_ROWS = 16
DIFF_CHUNKS = tuple(range(0, CTX + N + 1, 256))

VMEM_LIMIT = 60 * 1024 * 1024


def _cparams(sem):
    return pltpu.CompilerParams(dimension_semantics=sem, vmem_limit_bytes=VMEM_LIMIT)


def _mod_kernel(cin_ref, w_ref, b_ref, o_ref):
    cin = cin_ref[...]
    act = cin * jax.nn.sigmoid(cin)
    o_ref[...] = jnp.dot(act, w_ref[...], preferred_element_type=F32,
                         precision=lax.Precision.HIGHEST) + b_ref[...]


def _modulation(cin, w_mod, b_mod):
    tn = 1536
    return pl.pallas_call(
        _mod_kernel,
        grid=(DEPTH, 6 * D // tn),
        in_specs=[
            pl.BlockSpec((8, D), lambda l, j: (0, 0)),
            pl.BlockSpec((None, D, tn), lambda l, j: (l, 0, j)),
            pl.BlockSpec((None, 1, tn), lambda l, j: (l, 0, j)),
        ],
        out_specs=pl.BlockSpec((None, 8, tn), lambda l, j: (l, 0, j)),
        out_shape=jax.ShapeDtypeStruct((DEPTH, 8, 6 * D), F32),
        compiler_params=_cparams(("arbitrary", "arbitrary")),
        name="modulation",
    )(cin, w_mod, b_mod.reshape(DEPTH, 1, 6 * D))


def _norm_mod(x, g, shift, scale):
    ms = jnp.mean(x * x, axis=-1, keepdims=True)
    y = x * lax.rsqrt(ms + NORM_EPS) * g
    return y * (1.0 + scale) + shift


def _dot(a, b):
    return jnp.dot(a, b, preferred_element_type=F32)


def _dot_t(a, b):
    return lax.dot_general(a, b, (((1,), (1,)), ((), ())), preferred_element_type=F32)


def _softmax_pv(s, v_ext):
    m = jnp.max(s, axis=1, keepdims=True)
    p = jnp.exp2((s - m).astype(BF16))
    return _dot(p, v_ext)


def _lane_lo():
    return lax.broadcasted_iota(jnp.int32, (1, 2 * HD), 1) < HD


def _x_specs(xs):
    if isinstance(xs, tuple):
        return ([pl.BlockSpec((TM, D), lambda t: (jnp.minimum(t, NT_LAT - 1), 0)),
                 pl.BlockSpec((TM, D), lambda t: (jnp.maximum(t - NT_LAT, 0), 0))], list(xs))
    return [pl.BlockSpec((TM, D), lambda t: (t, 0))], [xs]


def _load_rows(refs):
    if len(refs) == 1:
        return refs[0][...]
    return jnp.where(pl.program_id(0) < NT_LAT, refs[0][...], refs[1][...])


def _layer_spec(shape, layer, **kw):
    return pl.BlockSpec((None,) + shape, lambda *_: (layer,) + (0,) * len(shape), **kw)


def _mod_spec(layer):
    return pl.BlockSpec((None, None, 6, D), lambda t: (layer, t // TILES_PER_BATCH, 0, 0))


def _proj_even_kernel(n_x, *refs):
    x_refs = refs[:n_x]
    mod_ref, g_ref, w_ref, cs_ref, fcs_ref, qkv_ref = refs[n_x:]
    h = _norm_mod(_load_rows(x_refs), g_ref[...], mod_ref[0:1, :], mod_ref[1:2, :]).astype(BF16)
    p = _dot(h, w_ref[...])
    f = p[:, :FW].astype(BF16)
    fcs_ref[...] = _dot(f, cs_ref[...]).astype(BF16)
    qkv_ref[:, :NAW] = (p[:, FW:FW + NAW] * Q_SCALE).astype(BF16)
    qkv_ref[:, NAW:] = p[:, FW + NAW:].astype(BF16)


def _proj_even(xs, mods, i, g_all, w_all, j, cs):
    nout = FW + 3 * NAW
    x_specs, x_args = _x_specs(xs)
    return pl.pallas_call(
        functools.partial(_proj_even_kernel, len(x_args)),
        grid=(NT,),
        in_specs=x_specs + [
            _mod_spec(i),
            _layer_spec((1, D), i),
            _layer_spec((D, nout), j),
            pl.BlockSpec((FW, 2 * FW), lambda t: (0, 0)),
        ],
        out_specs=[
            pl.BlockSpec((TM, 2 * FW), lambda t: (t, 0)),
            pl.BlockSpec((TM, 3 * NAW), lambda t: (t, 0)),
        ],
        out_shape=[
            jax.ShapeDtypeStruct((ROWS, 2 * FW), BF16),
            jax.ShapeDtypeStruct((ROWS, 3 * NAW), BF16),
        ],
        compiler_params=_cparams(("arbitrary",)),
        name="proj_even",
    )(*x_args, mods, g_all, w_all, cs)


def _proj_odd_kernel(x_ref, mod_ref, g_ref, w_ref, cos_ref, sa_ref, sb_ref, qkv_ref):
    h = _norm_mod(x_ref[...], g_ref[...], mod_ref[0:1, :], mod_ref[1:2, :]).astype(BF16)
    p = _dot(h, w_ref[...])
    cos = cos_ref[...]
    sa = sa_ref[...]
    sb = sb_ref[...]
    for c in range(2 * D // 128):
        xc = p[:, c * 128:(c + 1) * 128]
        r = xc * cos + pltpu.roll(xc, 112, 1) * sa + pltpu.roll(xc, 16, 1) * sb
        if c < D // 128:
            r = r * Q_SCALE
        qkv_ref[:, c * 128:(c + 1) * 128] = r.astype(BF16)
    qkv_ref[:, 2 * D:] = p[:, 2 * D:].astype(BF16)


def _proj_odd(xs, mods, i, g_all, w_all, j, cos, sa, sb):
    nout = 3 * D

    def rope_idx(t):
        return (jnp.where(t < NT_LAT, t % TILES_PER_BATCH, TILES_PER_BATCH), 0)

    return pl.pallas_call(
        _proj_odd_kernel,
        grid=(NT,),
        in_specs=[
            pl.BlockSpec((TM, D), lambda t: (t, 0)),
            _mod_spec(i),
            _layer_spec((1, D), i),
            _layer_spec((D, nout), j),
            pl.BlockSpec((TM, 128), rope_idx),
            pl.BlockSpec((TM, 128), rope_idx),
            pl.BlockSpec((TM, 128), rope_idx),
        ],
        out_specs=pl.BlockSpec((TM, nout), lambda t: (t, 0)),
        out_shape=jax.ShapeDtypeStruct((ROWS, nout), BF16),
        compiler_params=_cparams(("arbitrary",)),
        name="proj_odd",
    )(xs, mods, g_all, w_all, cos, sa, sb)


def _fourier_kernel(x_ref, tc_ref, ts_ref, ca_ref, sa_ref, o_ref):
    tc = tc_ref[...]
    ts = ts_ref[...]
    ca = ca_ref[...]
    sa = sa_ref[...]
    mc = (tc * ca - ts * sa).astype(BF16)
    ms = (ts * ca + tc * sa).astype(BF16)
    y = _dot(mc, x_ref[:, :FW]) - _dot(ms, x_ref[:, FW:])
    o_ref[...] = y.astype(BF16)


def _fourier(fcs, tc, ts, ca, sa):
    nti = N // TMF
    return pl.pallas_call(
        _fourier_kernel,
        grid=(B, nti),
        in_specs=[
            pl.BlockSpec((N, 2 * FW), lambda b, i: (b, 0)),
            pl.BlockSpec((TMF, N), lambda b, i: (0, 0)),
            pl.BlockSpec((TMF, N), lambda b, i: (0, 0)),
            pl.BlockSpec((None, 1, N), lambda b, i: (i, 0, 0)),
            pl.BlockSpec((None, 1, N), lambda b, i: (i, 0, 0)),
        ],
        out_specs=pl.BlockSpec((TMF, FW), lambda b, i: (b * nti + i, 0)),
        out_shape=jax.ShapeDtypeStruct((LAT, FW), BF16),
        compiler_params=_cparams(("arbitrary", "arbitrary")),
        name="fourier",
    )(fcs, tc, ts, ca, sa)


def _fourier_ctx_kernel(x_ref, m_ref, o_ref):
    y = _dot(m_ref[:, :CTX], x_ref[:, :FW]) - _dot(m_ref[:, CTX:], x_ref[:, FW:])
    o_ref[...] = y.astype(BF16)


def _fourier_ctx(fcs, mctx):
    return pl.pallas_call(
        _fourier_ctx_kernel,
        grid=(B,),
        in_specs=[
            pl.BlockSpec((CTX, 2 * FW), lambda b: (LAT // CTX + b, 0)),
            pl.BlockSpec((CTX, 2 * CTX), lambda b: (0, 0)),
        ],
        out_specs=pl.BlockSpec((CTX, FW), lambda b: (b, 0)),
        out_shape=jax.ShapeDtypeStruct((B * CTX, FW), BF16),
        compiler_params=_cparams(("arbitrary",)),
        name="fourier_ctx",
    )(fcs, mctx)


def _na_bias_kernel(rpb_ref, o_ref):
    j = pl.program_id(0)
    p = pl.program_id(1)
    cq = lax.broadcasted_iota(jnp.int32, (GW, 2 * GW), 0)
    lane = lax.broadcasted_iota(jnp.int32, (GW, 2 * GW), 1)
    ck = lane & (GW - 1)
    left = lane < GW
    cs = jnp.clip(cq - NA_KW // 2, 0, GW - NA_KW)
    col_ok = (ck >= cs) & (ck < cs + NA_KW)
    bidx = ck - cq + (NA_KW - 1)
    zero = jnp.zeros((GW, 2 * GW), F32)
    neg = jnp.full((GW, 2 * GW), NEG, F32)
    neg_l = jnp.where(left, neg, zero)
    neg_r = jnp.where(left, zero, neg)
    nb = 2 * NA_KW - 1
    na = 2 * NA_KH - 1
    for hh in range(2):
        base = ((j * 8 + 2 * p + hh) * na) * nb
        halves_l, halves_r = [], []
        for a in range(na):
            r = zero
            for b in range(nb):
                r = jnp.where(bidx == b, rpb_ref[base + a * nb + b], r)
            r = jnp.where(col_ok, r * LOG2E, neg)
            halves_l.append(jnp.where(left, r, zero))
            halves_r.append(jnp.where(left, zero, r))
        for cls, r0 in enumerate((0, QROWS, GH - QROWS)):
            start = min(max(r0 - NA_KH // 2, 0), GH - NA_WIN_ROWS)
            for rql in range(QROWS):
                rq = r0 + rql
                rs = min(max(rq - NA_KH // 2, 0), GH - NA_KH)
                for m in range(NA_WIN_ROWS // 2):
                    rk0 = start + 2 * m
                    rk1 = rk0 + 1
                    tl = halves_l[rk0 - rq + NA_KH - 1] if rs <= rk0 < rs + NA_KH else neg_l
                    tr = halves_r[rk1 - rq + NA_KH - 1] if rs <= rk1 < rs + NA_KH else neg_r
                    o_ref[cls, hh, rql * GW:(rql + 1) * GW, m * 128:(m + 1) * 128] = (tl + tr).astype(BF16)


def _na_bias(rpb_flat):
    n_even = (DEPTH + 1) // 2
    return pl.pallas_call(
        _na_bias_kernel,
        grid=(n_even, 4),
        in_specs=[pl.BlockSpec(memory_space=pltpu.SMEM)],
        out_specs=pl.BlockSpec((None, None, 3, 2, TQ, NA_WIN), lambda j, p: (j, p, 0, 0, 0, 0)),
        out_shape=jax.ShapeDtypeStruct((n_even, 4, 3, 2, TQ, NA_WIN), BF16),
        compiler_params=_cparams(("arbitrary", "arbitrary")),
        name="na_bias",
    )(rpb_flat)


def _na_kernel(q_ref, k_ref, v_ref, kc_ref, vc_ref, bias_ref, o_ref, s_buf, m_buf):
    nq = N // TQ
    nwc = NA_WIN // NA_CK
    lo = _lane_lo()
    kc = kc_ref[...]
    vc = vc_ref[...]

    def win_start(t):
        return pl.multiple_of(jnp.clip(t * QROWS - NA_KH // 2, 0, GH - NA_WIN_ROWS) * GW, 256)

    def stage(qk_t, qk_h, pv_t, pv_h):
        if qk_h is not None:
            q = q_ref[pl.ds(pl.multiple_of(qk_t * TQ, TQ), TQ), :]
            qh = jnp.where(lo if qk_h == 0 else jnp.logical_not(lo), q, jnp.zeros_like(q))
            k0 = win_start(qk_t)
            cls = jnp.where(qk_t == 0, 0, jnp.where(qk_t == nq - 1, 2, 1))
            m128 = None
        if pv_h is not None:
            m = jnp.max(m_buf[pv_h], axis=1, keepdims=True)
            v0 = win_start(pv_t)
            own = lo if pv_h == 0 else jnp.logical_not(lo)
            acc = None
        for c in range(nwc + 1):
            cols = slice(c * NA_CK, (c + 1) * NA_CK)
            if qk_h is not None:
                if c < nwc:
                    s = _dot_t(qh, k_ref[pl.ds(k0 + c * NA_CK, NA_CK), :])
                    s = s + bias_ref[cls, qk_h, :, cols].astype(F32)
                else:
                    s = _dot_t(qh, kc)
                s_buf[qk_h, :, cols] = s
                cm = jnp.maximum(s[:, :128], s[:, 128:])
                m128 = cm if m128 is None else jnp.maximum(m128, cm)
            if pv_h is not None:
                p = jnp.exp2((s_buf[pv_h, :, cols] - m).astype(BF16))
                vb = v_ref[pl.ds(v0 + c * NA_CK, NA_CK), :] if c < nwc else vc
                d = _dot(p, jnp.where(own, vb, jnp.ones_like(vb)))
                acc = d if acc is None else acc + d
        if qk_h is not None:
            m_buf[qk_h] = m128
        if pv_h is not None:
            return acc / pltpu.roll(acc, HD, 1)
        return None

    stage(0, 0, None, None)

    def tile_body(t, carry):
        o0 = stage(t, 1, t, 0)
        o1 = stage(jnp.minimum(t + 1, nq - 1), 0, t, 1)
        o_ref[pl.ds(pl.multiple_of(t * TQ, TQ), TQ), :] = jnp.where(lo, o0, o1).astype(BF16)
        return carry

    lax.fori_loop(0, nq, tile_body, 0)


def _na(qkv, bias, j):
    return pl.pallas_call(
        _na_kernel,
        grid=(4, B),
        in_specs=[
            pl.BlockSpec((N, 128), lambda p, b: (b, p)),
            pl.BlockSpec((N, 128), lambda p, b: (b, 4 + p)),
            pl.BlockSpec((N, 128), lambda p, b: (b, 8 + p)),
            pl.BlockSpec((CTX, 128), lambda p, b: (LAT // CTX + b, 4 + p)),
            pl.BlockSpec((CTX, 128), lambda p, b: (LAT // CTX + b, 8 + p)),
            pl.BlockSpec((None, None, 3, 2, TQ, NA_WIN), lambda p, b: (j, p, 0, 0, 0, 0)),
        ],
        out_specs=pl.BlockSpec((N, 128), lambda p, b: (b, p)),
        out_shape=jax.ShapeDtypeStruct((LAT, NAW), BF16),
        scratch_shapes=[
            pltpu.VMEM((2, TQ, NA_WIN + CTX), F32),
            pltpu.VMEM((2, TQ, 128), F32),
        ],
        compiler_params=_cparams(("arbitrary", "arbitrary")),
        name="na",
    )(qkv, qkv, qkv, qkv, qkv, bias)


def _na_ctx_kernel(q_ref, kc_ref, vc_ref, o_ref):
    q = q_ref[...]
    kc = kc_ref[...]
    vc = vc_ref[...]
    lo = _lane_lo()
    outs = []
    for hh in range(2):
        own = lo if hh == 0 else jnp.logical_not(lo)
        qh = jnp.where(own, q, jnp.zeros_like(q))
        o = _softmax_pv(_dot_t(qh, kc), jnp.where(own, vc, jnp.ones_like(vc)))
        outs.append(o / pltpu.roll(o, HD, 1))
    o_ref[...] = jnp.where(lo, outs[0], outs[1]).astype(BF16)


def _na_ctx(qkv):
    r0 = LAT // CTX
    return pl.pallas_call(
        _na_ctx_kernel,
        grid=(B, 4),
        in_specs=[
            pl.BlockSpec((CTX, 128), lambda b, p: (r0 + b, p)),
            pl.BlockSpec((CTX, 128), lambda b, p: (r0 + b, 4 + p)),
            pl.BlockSpec((CTX, 128), lambda b, p: (r0 + b, 8 + p)),
        ],
        out_specs=pl.BlockSpec((CTX, 128), lambda b, p: (b, p)),
        out_shape=jax.ShapeDtypeStruct((B * CTX, NAW), BF16),
        compiler_params=_cparams(("arbitrary", "arbitrary")),
        name="na_ctx",
    )(qkv, qkv, qkv)


def _diff_lambda(lq1_ref, lk1_ref, lq2_ref, lk2_ref, lam_init):
    s1 = jnp.sum(lq1_ref[...] * lk1_ref[...], axis=1, keepdims=True)
    s2 = jnp.sum(lq2_ref[...] * lk2_ref[...], axis=1, keepdims=True)
    return jnp.exp(s1) - jnp.exp(s2) + lam_init


def _diff_finish(o1, o2, lam, g, lam_init):
    o = o1 - lam * o2
    ms = jnp.mean(o * o, axis=-1, keepdims=True)
    return (o * lax.rsqrt(ms + SUBLN_EPS) * g * (1.0 - lam_init)).astype(BF16)


def _diff_kernel(lam_init, q_ref, kl_ref, vl_ref, kc_ref, vc_ref, lq1_ref, lk1_ref, lq2_ref, lk2_ref,
                 g_ref, o_ref, k_all, vt_ext, s_buf, m_buf):
    k_all[0:CTX, :] = kc_ref[...]
    k_all[CTX:, :] = kl_ref[...]
    vt_ext[0:128, 0:CTX] = vc_ref[...].astype(F32).T.astype(BF16)
    vt_ext[0:128, CTX:] = vl_ref[...].astype(F32).T.astype(BF16)
    vt_ext[128:, :] = jnp.ones((DIFF_SUM_ROWS, CTX + N), BF16)

    lam = _diff_lambda(lq1_ref, lk1_ref, lq2_ref, lk2_ref, lam_init)
    g_out = g_ref[...] * (1.0 - lam_init)
    lo = _lane_lo()
    nq = N // TQ

    def stage(qk_tile, qk_slot, pv_slot):
        if qk_slot is not None:
            q = q_ref[pl.ds(pl.multiple_of(qk_tile * TQ, TQ), TQ), :]
            qj = jnp.where(lo if qk_slot == 0 else jnp.logical_not(lo), q, jnp.zeros_like(q))
            m8 = None
        if pv_slot is not None:
            m = jnp.max(m_buf[pv_slot], axis=0, keepdims=True)
            acc = None

        def numerators(c, m_c):
            rows = slice(DIFF_CHUNKS[c], DIFF_CHUNKS[c + 1])
            return jnp.exp2((s_buf[pv_slot, rows, :] - m_c).astype(BF16))

        nck = len(DIFF_CHUNKS) - 1
        p_next = numerators(0, m) if pv_slot is not None else None
        cm_prev = None
        for c in range(nck):
            rows = slice(DIFF_CHUNKS[c], DIFF_CHUNKS[c + 1])
            m_c = m if pv_slot is not None else None
            if qk_slot is not None:
                if pv_slot is not None and cm_prev is not None:
                    m_c = jnp.maximum(m, jnp.minimum(cm_prev[0:1, :], m))
                s = _dot_t(k_all[rows, :], qj)
                s_buf[qk_slot, rows, :] = s
                cm_prev = jnp.max(s.reshape(-1, 8, TQ), axis=0)
                m8 = cm_prev if m8 is None else jnp.maximum(m8, cm_prev)
            if pv_slot is not None:
                p = p_next
                if c + 1 < nck:
                    p_next = numerators(c + 1, m_c)
                d = _dot(vt_ext[:, rows], p)
                acc = d if acc is None else acc + d
        if qk_slot is not None:
            m_buf[qk_slot] = m8
        if pv_slot is not None:
            return acc[0:128, :] / acc[128:129, :]
        return None

    stage(0, 0, None)

    def tile_body(t, carry):
        o1 = stage(t, 1, 0)
        o2 = stage(jnp.minimum(t + 1, nq - 1), 0, 1)
        o = o1 - lam * o2
        ms = jnp.mean(o * o, axis=0, keepdims=True)
        on = (o * lax.rsqrt(ms + SUBLN_EPS)).T
        o_ref[pl.ds(pl.multiple_of(t * TQ, TQ), TQ), :] = (on * g_out).astype(BF16)
        return carry

    lax.fori_loop(0, nq, tile_body, 0)


def _diff_ctx_kernel(lam_init, q_ref, kc_ref, vc_ref, lq1_ref, lk1_ref, lq2_ref, lk2_ref, g_ref, o_ref):
    lam = _diff_lambda(lq1_ref, lk1_ref, lq2_ref, lk2_ref, lam_init)
    q = q_ref[...]
    lo = _lane_lo()
    outs = []
    vc = vc_ref[...]
    v_ext = jnp.concatenate([vc, jnp.ones_like(vc)], axis=1)
    for jj in range(2):
        qj = jnp.where(lo if jj == 0 else jnp.logical_not(lo), q, jnp.zeros_like(q))
        o = _softmax_pv(_dot_t(qj, kc_ref[...]), v_ext)
        outs.append(o[:, :128] / o[:, 128:])
    o_ref[...] = _diff_finish(outs[0], outs[1], lam, g_ref[...], lam_init)


def _diff(qkv, lams, g, j, lam_init):
    nh = D // 128
    r0 = LAT // CTX
    return pl.pallas_call(
        functools.partial(_diff_kernel, lam_init),
        grid=(B, nh),
        in_specs=[
            pl.BlockSpec((N, 128), lambda b, h: (b, h)),
            pl.BlockSpec((N, 128), lambda b, h: (b, nh + h)),
            pl.BlockSpec((N, 128), lambda b, h: (b, 2 * nh + h)),
            pl.BlockSpec((CTX, 128), lambda b, h: (r0 + b, nh + h)),
            pl.BlockSpec((CTX, 128), lambda b, h: (r0 + b, 2 * nh + h)),
        ] + [_layer_spec((1, HD), j)] * 4 + [_layer_spec((1, 2 * HD), j)],
        out_specs=pl.BlockSpec((N, 128), lambda b, h: (b, h)),
        out_shape=jax.ShapeDtypeStruct((LAT, D), BF16),
        scratch_shapes=[
            pltpu.VMEM((CTX + N, 128), BF16),
            pltpu.VMEM((128 + DIFF_SUM_ROWS, CTX + N), BF16),
            pltpu.VMEM((2, CTX + N, TQ), F32),
            pltpu.VMEM((2, 8, TQ), F32),
        ],
        compiler_params=_cparams(("arbitrary", "arbitrary")),
        name="diff_attn",
    )(qkv, qkv, qkv, qkv, qkv, *lams, g)


def _diff_ctx(qkv, lams, g, j, lam_init):
    nh = D // 128
    r0 = LAT // CTX
    return pl.pallas_call(
        functools.partial(_diff_ctx_kernel, lam_init),
        grid=(B, nh),
        in_specs=[
            pl.BlockSpec((CTX, 128), lambda b, h: (r0 + b, h)),
            pl.BlockSpec((CTX, 128), lambda b, h: (r0 + b, nh + h)),
            pl.BlockSpec((CTX, 128), lambda b, h: (r0 + b, 2 * nh + h)),
        ] + [_layer_spec((1, HD), j)] * 4 + [_layer_spec((1, 2 * HD), j)],
        out_specs=pl.BlockSpec((CTX, 128), lambda b, h: (b, h)),
        out_shape=jax.ShapeDtypeStruct((B * CTX, D), BF16),
        compiler_params=_cparams(("arbitrary", "arbitrary")),
        name="diff_attn_ctx",
    )(qkv, qkv, qkv, *lams, g)


def _post_kernel(last, n_x, widths, *refs):
    n_att = len(widths) * (1 if last else 2)
    x_refs = refs[:n_x]
    att_refs = refs[n_x:n_x + n_att]
    mod_ref, g_ref, wo_ref, wi_ref, w2_ref, fg_ref, o_ref = refs[n_x + n_att:]
    x = _load_rows(x_refs)
    is_latent = pl.program_id(0) < NT_LAT
    y = None
    off = 0
    for i, w in enumerate(widths):
        if last:
            a = att_refs[i][...]
        else:
            a = jnp.where(is_latent, att_refs[2 * i][...], att_refs[2 * i + 1][...])
        d = _dot(a, wo_ref[off:off + w, :])
        y = d if y is None else y + d
        off += w
    x2 = x + mod_ref[2:3, :] * y
    h = _norm_mod(x2, g_ref[...], mod_ref[3:4, :], mod_ref[4:5, :]).astype(BF16)
    acc = jnp.zeros((TM, D), F32)
    for c in range(MLP_H // D):
        u = jnp.maximum(_dot(h, wi_ref[:, c * D:(c + 1) * D]), 0.0)
        acc = acc + _dot((u * u).astype(BF16), w2_ref[c * D:(c + 1) * D, :])
    x3 = x2 + mod_ref[5:6, :] * acc
    if last:
        ms = jnp.mean(x3 * x3, axis=-1, keepdims=True)
        x3 = x3 * lax.rsqrt(ms + NORM_EPS) * fg_ref[...]
    o_ref[...] = x3


def _post(xs, att, mods, i, g_all, wo_all, j, wi_all, w2_all, fg, last):
    nt = NT_LAT if last else NT
    x_specs, x_args = _x_specs(xs)
    widths = tuple(a_lat.shape[1] for a_lat, _ in att)
    assert sum(widths) == D
    att_specs, att_args = [], []
    for (a_lat, a_ctx), w in zip(att, widths):
        att_specs.append(pl.BlockSpec((TM, w), lambda t: (jnp.minimum(t, NT_LAT - 1), 0)))
        att_args.append(a_lat)
        if not last:
            att_specs.append(pl.BlockSpec((TM, w), lambda t: (jnp.maximum(t - NT_LAT, 0), 0)))
            att_args.append(a_ctx)
    once = dict(pipeline_mode=pl.Buffered(1))
    return pl.pallas_call(
        functools.partial(_post_kernel, last, len(x_args), widths),
        grid=(nt,),
        in_specs=x_specs + att_specs + [
            _mod_spec(i),
            _layer_spec((1, D), i),
            _layer_spec((D, D), j, **once),
            _layer_spec((D, MLP_H), i, **once),
            _layer_spec((MLP_H, D), i, **once),
            pl.BlockSpec((1, D), lambda t: (0, 0)),
        ],
        out_specs=pl.BlockSpec((TM, D), lambda t: (t, 0)),
        out_shape=jax.ShapeDtypeStruct((nt * TM, D), F32),
        compiler_params=_cparams(("arbitrary",)),
        name="post_last" if last else "post",
    )(*x_args, *att_args, mods, g_all, wo_all, wi_all, w2_all, fg)


def _channel_dft():
    n = np.arange(FD)
    ang = 2.0 * np.pi * ((n[:, None] * n[None, :]) % FD) / FD
    c = np.cos(ang) / math.sqrt(FD)
    s = np.sin(ang) / math.sqrt(FD)
    m = np.zeros((FW, 2 * FW), np.float32)
    for g in range(FG):
        m[g * FD:(g + 1) * FD, g * FD:(g + 1) * FD] = c
        m[g * FD:(g + 1) * FD, FW + g * FD:FW + (g + 1) * FD] = s
    return jnp.asarray(m).astype(BF16)


def _ctx_dft():
    n = np.arange(CTX)
    ang = 2.0 * np.pi * ((n[:, None] * n[None, :]) % CTX) / CTX
    m = np.concatenate([np.cos(ang), np.sin(ang)], axis=1) / math.sqrt(CTX)
    return jnp.asarray(m.astype(np.float32)).astype(BF16)


def _position_dft_tables():
    l = jnp.arange(N, dtype=jnp.int32)[None, :]
    dk = jnp.arange(TMF, dtype=jnp.int32)[:, None]
    ang = ((dk * l) % N).astype(F32) * (2.0 * math.pi / N)
    k0 = (jnp.arange(N // TMF, dtype=jnp.int32) * TMF)[:, None]
    ang0 = ((k0 * l) % N).astype(F32) * (2.0 * math.pi / N)
    scale = 1.0 / math.sqrt(N)
    ca = (jnp.cos(ang0) * scale).reshape(N // TMF, 1, N)
    sa = (jnp.sin(ang0) * scale).reshape(N // TMF, 1, N)
    return jnp.cos(ang), jnp.sin(ang), ca, sa


def _rope_tables():
    t = jnp.arange(N)
    rows, cols = t // GW, t % GW
    nf = HD // 4
    inv_freq = ROPE_THETA ** (-jnp.arange(nf, dtype=F32) / nf)
    lane = np.arange(128)
    d = lane % HD
    use_col = d >= HD // 2
    first = (d % (HD // 2)) < nf
    pos = jnp.where(use_col[None, :], cols[:, None], rows[:, None]).astype(F32)
    ang = pos * inv_freq[d % nf][None, :]
    cos, sin = jnp.cos(ang), jnp.sin(ang)
    sa = jnp.where(first[None, :], -sin, 0.0)
    sb = jnp.where(first[None, :], 0.0, sin)
    pad = jnp.zeros((TM, 128), F32)
    return (jnp.concatenate([cos, pad + 1.0]), jnp.concatenate([sa, pad]), jnp.concatenate([sb, pad]))


def kernel(x, c, ctx, c_ctx, w_mod, b_mod, norm_mix_g, norm_mlp_g, w_mlp_in, w_mlp_out, w_in_ab, w_out_ab,
           na_rpb, w_qkv_diff, w_out_diff, diff_lq1, diff_lk1, diff_lq2, diff_lk2, diff_subln_g,
           final_norm_g):
    assert x.shape == (B, N, D) and ctx.shape == (B, CTX, D)
    xs = (x.reshape(LAT, D), ctx.reshape(B * CTX, D))
    cin = jnp.concatenate([c, c_ctx[None, :], jnp.zeros((8 - B - 1, D), F32)], axis=0)
    mods = _modulation(cin, w_mod, b_mod).reshape(DEPTH, 8, 6, D)

    cs = _channel_dft()
    mctx = _ctx_dft()
    tc, ts, ca, sa = _position_dft_tables()
    rope = _rope_tables()
    bias = _na_bias(na_rpb.reshape(-1))
    fg = final_norm_g.reshape(1, D)

    g_mix = norm_mix_g.reshape(DEPTH, 1, D)
    g_mlp = norm_mlp_g.reshape(DEPTH, 1, D)
    w_ab, wo_ab = w_in_ab.astype(BF16), w_out_ab.astype(BF16)
    w_qkv, wo_diff = w_qkv_diff.astype(BF16), w_out_diff.astype(BF16)
    w_in, w_out = w_mlp_in.astype(BF16), w_mlp_out.astype(BF16)
    lams = [v.reshape(-1, 1, HD) for v in (diff_lq1, diff_lk1, diff_lq2, diff_lk2)]
    g_sub = diff_subln_g.reshape(-1, 1, 2 * HD)

    for i in range(DEPTH):
        last = i == DEPTH - 1
        j = i // 2
        if i % 2 == 0:
            fcs, qkv = _proj_even(xs, mods, i, g_mix, w_ab, j, cs)
            att = [(_fourier(fcs, tc, ts, ca, sa), None if last else _fourier_ctx(fcs, mctx)),
                   (_na(qkv, bias, j), None if last else _na_ctx(qkv))]
            wo = wo_ab
        else:
            lam_init = 0.8 - 0.6 * math.exp(-0.3 * i)
            qkv = _proj_odd(xs, mods, i, g_mix, w_qkv, j, *rope)
            att = [(_diff(qkv, lams, g_sub, j, lam_init),
                    None if last else _diff_ctx(qkv, lams, g_sub, j, lam_init))]
            wo = wo_diff
        xs = _post(xs, att, mods, i, g_mlp, wo, j, w_in, w_out, fg, last)
    return xs.reshape(B, N, D)
```

```python
import functools
import math

import numpy as np
import jax
import jax.numpy as jnp
from jax import lax
from jax.experimental import pallas as pl
from jax.experimental.pallas import tpu as pltpu

F32 = jnp.float32
BF16 = jnp.bfloat16

D = 1024
B = 4
N = 4096
CTX = 256
DEPTH = 4
GW = 64
GH = N // GW
FW = 512
FG = 4
FD = 128
NAW = 512
HD = 64
NA_KH = 8
NA_KW = 16
MLP_H = 4 * D
NORM_EPS = 1e-6
SUBLN_EPS = 1e-5
NEG = -1e30
ROPE_THETA = 10000.0
LOG2E = 1.4426950408889634
Q_SCALE = HD ** -0.5 * LOG2E

TM = 512
LAT = B * N
ROWS = LAT + B * CTX
NT = ROWS // TM
NT_LAT = LAT // TM
TILES_PER_BATCH = N // TM
TQ = 512
QROWS = TQ // GW
NA_WIN_ROWS = 16
NA_WIN = NA_WIN_ROWS * GW
NA_CK = 256
TMF = 512
DIFF_SUM_ROWS = 16
DIFF_CHUNKS = tuple(range(0, CTX + N + 1, 256))

VMEM_LIMIT = 60 * 1024 * 1024


def _cparams(sem):
    return pltpu.CompilerParams(dimension_semantics=sem, vmem_limit_bytes=VMEM_LIMIT)


def _mod_kernel(cin_ref, w_ref, b_ref, o_ref):
    cin = cin_ref[...]
    act = cin * jax.nn.sigmoid(cin)
    o_ref[...] = jnp.dot(act, w_ref[...], preferred_element_type=F32,
                         precision=lax.Precision.HIGHEST) + b_ref[...]


def _modulation(cin, w_mod, b_mod):
    tn = 3072
    return pl.pallas_call(
        _mod_kernel,
        grid=(DEPTH, 6 * D // tn),
        in_specs=[
            pl.BlockSpec((8, D), lambda l, j: (0, 0)),
            pl.BlockSpec((None, D, tn), lambda l, j: (l, 0, j)),
            pl.BlockSpec((None, 1, tn), lambda l, j: (l, 0, j)),
        ],
        out_specs=pl.BlockSpec((None, 8, tn), lambda l, j: (l, 0, j)),
        out_shape=jax.ShapeDtypeStruct((DEPTH, 8, 6 * D), F32),
        compiler_params=_cparams(("arbitrary", "arbitrary")),
        name="modulation",
    )(cin, w_mod, b_mod.reshape(DEPTH, 1, 6 * D))


def _norm_mod(x, g, shift, scale):
    ms = jnp.mean(x * x, axis=-1, keepdims=True)
    y = x * lax.rsqrt(ms + NORM_EPS) * g
    return y * (1.0 + scale) + shift


def _dot(a, b):
    return jnp.dot(a, b, preferred_element_type=F32)


def _dot_t(a, b):
    return lax.dot_general(a, b, (((1,), (1,)), ((), ())), preferred_element_type=F32)


def _softmax_pv(s, v_ext):
    m = jnp.max(s, axis=1, keepdims=True)
    p = jnp.exp2((s - m).astype(BF16))
    return _dot(p, v_ext)


def _lane_lo():
    return lax.broadcasted_iota(jnp.int32, (1, 2 * HD), 1) < HD


def _x_specs(xs):
    if isinstance(xs, tuple):
        return ([pl.BlockSpec((TM, D), lambda t: (jnp.minimum(t, NT_LAT - 1), 0)),
                 pl.BlockSpec((TM, D), lambda t: (jnp.maximum(t - NT_LAT, 0), 0))], list(xs))
    return [pl.BlockSpec((TM, D), lambda t: (t, 0))], [xs]


def _load_rows(refs):
    if len(refs) == 1:
        return refs[0][...]
    return jnp.where(pl.program_id(0) < NT_LAT, refs[0][...], refs[1][...])


def _layer_spec(shape, layer, **kw):
    return pl.BlockSpec((None,) + shape, lambda *_: (layer,) + (0,) * len(shape), **kw)


def _mod_spec(layer):
    return pl.BlockSpec((None, None, 6, D), lambda t: (layer, t // TILES_PER_BATCH, 0, 0))


def _proj_even_kernel(n_x, *refs):
    x_refs = refs[:n_x]
    mod_ref, g_ref, w_ref, cs_ref, fcs_ref, qkv_ref = refs[n_x:]
    h = _norm_mod(_load_rows(x_refs), g_ref[...], mod_ref[0:1, :], mod_ref[1:2, :]).astype(BF16)
    p = _dot(h, w_ref[...])
    f = p[:, :FW].astype(BF16)
    fcs_ref[...] = _dot(f, cs_ref[...]).astype(BF16)
    qkv_ref[:, :NAW] = (p[:, FW:FW + NAW] * Q_SCALE).astype(BF16)
    qkv_ref[:, NAW:] = p[:, FW + NAW:].astype(BF16)


def _proj_even(xs, mods, i, g_all, w_all, j, cs):
    nout = FW + 3 * NAW
    x_specs, x_args = _x_specs(xs)
    return pl.pallas_call(
        functools.partial(_proj_even_kernel, len(x_args)),
        grid=(NT,),
        in_specs=x_specs + [
            _mod_spec(i),
            _layer_spec((1, D), i),
            _layer_spec((D, nout), j),
            pl.BlockSpec((FW, 2 * FW), lambda t: (0, 0)),
        ],
        out_specs=[
            pl.BlockSpec((TM, 2 * FW), lambda t: (t, 0)),
            pl.BlockSpec((TM, 3 * NAW), lambda t: (t, 0)),
        ],
        out_shape=[
            jax.ShapeDtypeStruct((ROWS, 2 * FW), BF16),
            jax.ShapeDtypeStruct((ROWS, 3 * NAW), BF16),
        ],
        compiler_params=_cparams(("arbitrary",)),
        name="proj_even",
    )(*x_args, mods, g_all, w_all, cs)


def _proj_odd_kernel(x_ref, mod_ref, g_ref, w_ref, cos_ref, sa_ref, sb_ref, qkv_ref):
    h = _norm_mod(x_ref[...], g_ref[...], mod_ref[0:1, :], mod_ref[1:2, :]).astype(BF16)
    p = _dot(h, w_ref[...])
    cos = cos_ref[...]
    sa = sa_ref[...]
    sb = sb_ref[...]
    for c in range(2 * D // 128):
        xc = p[:, c * 128:(c + 1) * 128]
        r = xc * cos + pltpu.roll(xc, 112, 1) * sa + pltpu.roll(xc, 16, 1) * sb
        if c < D // 128:
            r = r * Q_SCALE
        qkv_ref[:, c * 128:(c + 1) * 128] = r.astype(BF16)
    qkv_ref[:, 2 * D:] = p[:, 2 * D:].astype(BF16)


def _proj_odd(xs, mods, i, g_all, w_all, j, cos, sa, sb):
    nout = 3 * D

    def rope_idx(t):
        return (jnp.where(t < NT_LAT, t % TILES_PER_BATCH, TILES_PER_BATCH), 0)

    return pl.pallas_call(
        _proj_odd_kernel,
        grid=(NT,),
        in_specs=[
            pl.BlockSpec((TM, D), lambda t: (t, 0)),
            _mod_spec(i),
            _layer_spec((1, D), i),
            _layer_spec((D, nout), j),
            pl.BlockSpec((TM, 128), rope_idx),
            pl.BlockSpec((TM, 128), rope_idx),
            pl.BlockSpec((TM, 128), rope_idx),
        ],
        out_specs=pl.BlockSpec((TM, nout), lambda t: (t, 0)),
        out_shape=jax.ShapeDtypeStruct((ROWS, nout), BF16),
        compiler_params=_cparams(("arbitrary",)),
        name="proj_odd",
    )(xs, mods, g_all, w_all, cos, sa, sb)


def _fourier_kernel(x_ref, xr_ref, tc_ref, ts_ref, ca_ref, sa_ref, o_ref, eo):
    half = N // 2

    @pl.when(pl.program_id(1) == 0)
    def _():
        xf = x_ref[0:half, :].astype(F32)
        xr = xr_ref[...].astype(F32)
        first = lax.broadcasted_iota(jnp.int32, (half, FW), 0) == 0
        eo[:, :FW] = jnp.where(first, xf[:, :FW], xf[:, :FW] + xr[:, :FW]).astype(BF16)
        eo[:, FW:] = (xf[:, FW:] - xr[:, FW:]).astype(BF16)

    tc = tc_ref[...]
    ts = ts_ref[...]
    ca = ca_ref[...]
    sa = sa_ref[...]
    mc = (tc * ca - ts * sa).astype(BF16)
    ms = (ts * ca + tc * sa).astype(BF16)
    y = _dot(mc, eo[:, :FW]) - _dot(ms, eo[:, FW:])
    odd = (lax.broadcasted_iota(jnp.int32, (TMF, FW), 0) & 1) == 1
    mid = x_ref[half:half + 1, :FW].astype(F32) * (1.0 / math.sqrt(N))
    o_ref[...] = (y + jnp.where(odd, -mid, mid)).astype(BF16)


def _fourier(fcs, tc, ts, ca, sa):
    nti = N // TMF
    half = N // 2
    lat = fcs[:LAT].reshape(B, N, 2 * FW)
    rev = jnp.roll(jnp.flip(lat, axis=1), 1, axis=1)[:, :half].reshape(B * half, 2 * FW)
    return pl.pallas_call(
        _fourier_kernel,
        grid=(B, nti),
        in_specs=[
            pl.BlockSpec((N, 2 * FW), lambda b, i: (b, 0)),
            pl.BlockSpec((half, 2 * FW), lambda b, i: (b, 0)),
            pl.BlockSpec((TMF, half), lambda b, i: (0, 0)),
            pl.BlockSpec((TMF, half), lambda b, i: (0, 0)),
            pl.BlockSpec((None, 1, half), lambda b, i: (i, 0, 0)),
            pl.BlockSpec((None, 1, half), lambda b, i: (i, 0, 0)),
        ],
        out_specs=pl.BlockSpec((TMF, FW), lambda b, i: (b * nti + i, 0)),
        out_shape=jax.ShapeDtypeStruct((LAT, FW), BF16),
        scratch_shapes=[pltpu.VMEM((half, 2 * FW), BF16)],
        compiler_params=_cparams(("arbitrary", "arbitrary")),
        name="fourier",
    )(fcs, rev, tc, ts, ca, sa)


def _fourier_ctx_kernel(x_ref, m_ref, o_ref):
    y = _dot(m_ref[:, :CTX], x_ref[:, :FW]) - _dot(m_ref[:, CTX:], x_ref[:, FW:])
    o_ref[...] = y.astype(BF16)


def _fourier_ctx(fcs, mctx):
    return pl.pallas_call(
        _fourier_ctx_kernel,
        grid=(B,),
        in_specs=[
            pl.BlockSpec((CTX, 2 * FW), lambda b: (LAT // CTX + b, 0)),
            pl.BlockSpec((CTX, 2 * CTX), lambda b: (0, 0)),
        ],
        out_specs=pl.BlockSpec((CTX, FW), lambda b: (b, 0)),
        out_shape=jax.ShapeDtypeStruct((B * CTX, FW), BF16),
        compiler_params=_cparams(("arbitrary",)),
        name="fourier_ctx",
    )(fcs, mctx)


def _na_bias_kernel(rpb_ref, o_ref):
    j = pl.program_id(0)
    p = pl.program_id(1)
    cq = lax.broadcasted_iota(jnp.int32, (GW, 2 * GW), 0)
    lane = lax.broadcasted_iota(jnp.int32, (GW, 2 * GW), 1)
    ck = lane & (GW - 1)
    left = lane < GW
    cs = jnp.clip(cq - NA_KW // 2, 0, GW - NA_KW)
    col_ok = (ck >= cs) & (ck < cs + NA_KW)
    bidx = ck - cq + (NA_KW - 1)
    zero = jnp.zeros((GW, 2 * GW), F32)
    neg = jnp.full((GW, 2 * GW), NEG, F32)
    neg_l = jnp.where(left, neg, zero)
    neg_r = jnp.where(left, zero, neg)
    nb = 2 * NA_KW - 1
    na = 2 * NA_KH - 1
    for hh in range(2):
        base = ((j * 8 + 2 * p + hh) * na) * nb
        halves_l, halves_r = [], []
        for a in range(na):
            r = zero
            for b in range(nb):
                r = jnp.where(bidx == b, rpb_ref[base + a * nb + b], r)
            r = jnp.where(col_ok, r * LOG2E, neg)
            halves_l.append(jnp.where(left, r, zero))
            halves_r.append(jnp.where(left, zero, r))
        for cls, r0 in enumerate((0, QROWS, GH - QROWS)):
            start = min(max(r0 - NA_KH // 2, 0), GH - NA_WIN_ROWS)
            for rql in range(QROWS):
                rq = r0 + rql
                rs = min(max(rq - NA_KH // 2, 0), GH - NA_KH)
                for m in range(NA_WIN_ROWS // 2):
                    rk0 = start + 2 * m
                    rk1 = rk0 + 1
                    tl = halves_l[rk0 - rq + NA_KH - 1] if rs <= rk0 < rs + NA_KH else neg_l
                    tr = halves_r[rk1 - rq + NA_KH - 1] if rs <= rk1 < rs + NA_KH else neg_r
                    o_ref[cls, hh, rql * GW:(rql + 1) * GW, m * 128:(m + 1) * 128] = (tl + tr).astype(BF16)


def _na_bias(rpb_flat):
    n_even = (DEPTH + 1) // 2
    return pl.pallas_call(
        _na_bias_kernel,
        grid=(n_even, 4),
        in_specs=[pl.BlockSpec(memory_space=pltpu.SMEM)],
        out_specs=pl.BlockSpec((None, None, 3, 2, TQ, NA_WIN), lambda j, p: (j, p, 0, 0, 0, 0)),
        out_shape=jax.ShapeDtypeStruct((n_even, 4, 3, 2, TQ, NA_WIN), BF16),
        compiler_params=_cparams(("arbitrary", "arbitrary")),
        name="na_bias",
    )(rpb_flat)


def _na_kernel(q_ref, k_ref, v_ref, kc_ref, vc_ref, bias_ref, o_ref, s_buf, m_buf):
    nq = N // TQ
    nwc = NA_WIN // NA_CK
    lo = _lane_lo()
    kc = kc_ref[...]
    vc = vc_ref[...]

    def win_start(t):
        return pl.multiple_of(jnp.clip(t * QROWS - NA_KH // 2, 0, GH - NA_WIN_ROWS) * GW, 256)

    def stage(qk_t, qk_h, pv_t, pv_h):
        if qk_h is not None:
            q = q_ref[pl.ds(pl.multiple_of(qk_t * TQ, TQ), TQ), :]
            qh = jnp.where(lo if qk_h == 0 else jnp.logical_not(lo), q, jnp.zeros_like(q))
            k0 = win_start(qk_t)
            cls = jnp.where(qk_t == 0, 0, jnp.where(qk_t == nq - 1, 2, 1))
            m128 = None
        if pv_h is not None:
            m = jnp.max(m_buf[pv_h], axis=1, keepdims=True)
            v0 = win_start(pv_t)
            own = lo if pv_h == 0 else jnp.logical_not(lo)
            acc = None
        for c in range(nwc + 1):
            cols = slice(c * NA_CK, (c + 1) * NA_CK)
            if qk_h is not None:
                if c < nwc:
                    s = _dot_t(qh, k_ref[pl.ds(k0 + c * NA_CK, NA_CK), :])
                    s = s + bias_ref[cls, qk_h, :, cols].astype(F32)
                else:
                    s = _dot_t(qh, kc)
                s_buf[qk_h, :, cols] = s
                cm = jnp.maximum(s[:, :128], s[:, 128:])
                m128 = cm if m128 is None else jnp.maximum(m128, cm)
            if pv_h is not None:
                p = jnp.exp2((s_buf[pv_h, :, cols] - m).astype(BF16))
                vb = v_ref[pl.ds(v0 + c * NA_CK, NA_CK), :] if c < nwc else vc
                d = _dot(p, jnp.where(own, vb, jnp.ones_like(vb)))
                acc = d if acc is None else acc + d
        if qk_h is not None:
            m_buf[qk_h] = m128
        if pv_h is not None:
            return acc / pltpu.roll(acc, HD, 1)
        return None

    stage(0, 0, None, None)

    def tile_body(t, carry):
        o0 = stage(t, 1, t, 0)
        o1 = stage(jnp.minimum(t + 1, nq - 1), 0, t, 1)
        o_ref[pl.ds(pl.multiple_of(t * TQ, TQ), TQ), :] = jnp.where(lo, o0, o1).astype(BF16)
        return carry

    lax.fori_loop(0, nq, tile_body, 0)


def _na(qkv, bias, j):
    return pl.pallas_call(
        _na_kernel,
        grid=(4, B),
        in_specs=[
            pl.BlockSpec((N, 128), lambda p, b: (b, p)),
            pl.BlockSpec((N, 128), lambda p, b: (b, 4 + p)),
            pl.BlockSpec((N, 128), lambda p, b: (b, 8 + p)),
            pl.BlockSpec((CTX, 128), lambda p, b: (LAT // CTX + b, 4 + p)),
            pl.BlockSpec((CTX, 128), lambda p, b: (LAT // CTX + b, 8 + p)),
            pl.BlockSpec((None, None, 3, 2, TQ, NA_WIN), lambda p, b: (j, p, 0, 0, 0, 0)),
        ],
        out_specs=pl.BlockSpec((N, 128), lambda p, b: (b, p)),
        out_shape=jax.ShapeDtypeStruct((LAT, NAW), BF16),
        scratch_shapes=[
            pltpu.VMEM((2, TQ, NA_WIN + CTX), F32),
            pltpu.VMEM((2, TQ, 128), F32),
        ],
        compiler_params=_cparams(("arbitrary", "arbitrary")),
        name="na",
    )(qkv, qkv, qkv, qkv, qkv, bias)


def _na_ctx_kernel(q_ref, kc_ref, vc_ref, o_ref):
    q = q_ref[...]
    kc = kc_ref[...]
    vc = vc_ref[...]
    lo = _lane_lo()
    outs = []
    for hh in range(2):
        own = lo if hh == 0 else jnp.logical_not(lo)
        qh = jnp.where(own, q, jnp.zeros_like(q))
        o = _softmax_pv(_dot_t(qh, kc), jnp.where(own, vc, jnp.ones_like(vc)))
        outs.append(o / pltpu.roll(o, HD, 1))
    o_ref[...] = jnp.where(lo, outs[0], outs[1]).astype(BF16)


def _na_ctx(qkv):
    r0 = LAT // CTX
    return pl.pallas_call(
        _na_ctx_kernel,
        grid=(B, 4),
        in_specs=[
            pl.BlockSpec((CTX, 128), lambda b, p: (r0 + b, p)),
            pl.BlockSpec((CTX, 128), lambda b, p: (r0 + b, 4 + p)),
            pl.BlockSpec((CTX, 128), lambda b, p: (r0 + b, 8 + p)),
        ],
        out_specs=pl.BlockSpec((CTX, 128), lambda b, p: (b, p)),
        out_shape=jax.ShapeDtypeStruct((B * CTX, NAW), BF16),
        compiler_params=_cparams(("arbitrary", "arbitrary")),
        name="na_ctx",
    )(qkv, qkv, qkv)


def _diff_lambda(lq1_ref, lk1_ref, lq2_ref, lk2_ref, lam_init):
    s1 = jnp.sum(lq1_ref[...] * lk1_ref[...], axis=1, keepdims=True)
    s2 = jnp.sum(lq2_ref[...] * lk2_ref[...], axis=1, keepdims=True)
    return jnp.exp(s1) - jnp.exp(s2) + lam_init


def _diff_finish(o1, o2, lam, g, lam_init):
    o = o1 - lam * o2
    ms = jnp.mean(o * o, axis=-1, keepdims=True)
    return (o * lax.rsqrt(ms + SUBLN_EPS) * g * (1.0 - lam_init)).astype(BF16)


def _diff_kernel(lam_init, q_ref, kl_ref, vl_ref, kc_ref, vc_ref, lq1_ref, lk1_ref, lq2_ref, lk2_ref,
                 g_ref, o_ref, k_all, vt_ext, s_buf, m_buf):
    k_all[0:CTX, :] = kc_ref[...]
    k_all[CTX:, :] = kl_ref[...]
    vt_ext[0:128, 0:CTX] = vc_ref[...].astype(F32).T.astype(BF16)
    vt_ext[0:128, CTX:] = vl_ref[...].astype(F32).T.astype(BF16)
    vt_ext[128:, :] = jnp.ones((DIFF_SUM_ROWS, CTX + N), BF16)

    lam = _diff_lambda(lq1_ref, lk1_ref, lq2_ref, lk2_ref, lam_init)
    g_out = g_ref[...] * (1.0 - lam_init)
    lo = _lane_lo()
    nq = N // TQ

    def stage(qk_tile, qk_slot, pv_slot):
        if qk_slot is not None:
            q = q_ref[pl.ds(pl.multiple_of(qk_tile * TQ, TQ), TQ), :]
            qj = jnp.where(lo if qk_slot == 0 else jnp.logical_not(lo), q, jnp.zeros_like(q))
            m8 = None
        if pv_slot is not None:
            m = jnp.max(m_buf[pv_slot], axis=0, keepdims=True)
            acc = None

        def numerators(c, m_c):
            rows = slice(DIFF_CHUNKS[c], DIFF_CHUNKS[c + 1])
            return jnp.exp2((s_buf[pv_slot, rows, :] - m_c).astype(BF16))

        nck = len(DIFF_CHUNKS) - 1
        p_next = numerators(0, m) if pv_slot is not None else None
        cm_prev = None
        for c in range(nck):
            rows = slice(DIFF_CHUNKS[c], DIFF_CHUNKS[c + 1])
            m_c = m if pv_slot is not None else None
            if qk_slot is not None:
                if pv_slot is not None and cm_prev is not None:
                    m_c = jnp.maximum(m, jnp.minimum(cm_prev[0:1, :], m))
                s = _dot_t(k_all[rows, :], qj)
                s_buf[qk_slot, rows, :] = s
                cm_prev = jnp.max(s.reshape(-1, 8, TQ), axis=0)
                m8 = cm_prev if m8 is None else jnp.maximum(m8, cm_prev)
            if pv_slot is not None:
                p = p_next
                if c + 1 < nck:
                    p_next = numerators(c + 1, m_c)
                d = _dot(vt_ext[:, rows], p)
                acc = d if acc is None else acc + d
        if qk_slot is not None:
            m_buf[qk_slot] = m8
        if pv_slot is not None:
            return acc[0:128, :] / acc[128:129, :]
        return None

    stage(0, 0, None)

    def tile_body(t, carry):
        o1 = stage(t, 1, 0)
        o2 = stage(jnp.minimum(t + 1, nq - 1), 0, 1)
        o = o1 - lam * o2
        ms = jnp.mean(o * o, axis=0, keepdims=True)
        on = (o * lax.rsqrt(ms + SUBLN_EPS)).T
        o_ref[pl.ds(pl.multiple_of(t * TQ, TQ), TQ), :] = (on * g_out).astype(BF16)
        return carry

    lax.fori_loop(0, nq, tile_body, 0)


def _diff_ctx_kernel(lam_init, q_ref, kc_ref, vc_ref, lq1_ref, lk1_ref, lq2_ref, lk2_ref, g_ref, o_ref):
    lam = _diff_lambda(lq1_ref, lk1_ref, lq2_ref, lk2_ref, lam_init)
    q = q_ref[...]
    lo = _lane_lo()
    outs = []
    vc = vc_ref[...]
    v_ext = jnp.concatenate([vc, jnp.ones_like(vc)], axis=1)
    for jj in range(2):
        qj = jnp.where(lo if jj == 0 else jnp.logical_not(lo), q, jnp.zeros_like(q))
        o = _softmax_pv(_dot_t(qj, kc_ref[...]), v_ext)
        outs.append(o[:, :128] / o[:, 128:])
    o_ref[...] = _diff_finish(outs[0], outs[1], lam, g_ref[...], lam_init)


def _diff(qkv, lams, g, j, lam_init):
    nh = D // 128
    r0 = LAT // CTX
    return pl.pallas_call(
        functools.partial(_diff_kernel, lam_init),
        grid=(B, nh),
        in_specs=[
            pl.BlockSpec((N, 128), lambda b, h: (b, h)),
            pl.BlockSpec((N, 128), lambda b, h: (b, nh + h)),
            pl.BlockSpec((N, 128), lambda b, h: (b, 2 * nh + h)),
            pl.BlockSpec((CTX, 128), lambda b, h: (r0 + b, nh + h)),
            pl.BlockSpec((CTX, 128), lambda b, h: (r0 + b, 2 * nh + h)),
        ] + [_layer_spec((1, HD), j)] * 4 + [_layer_spec((1, 2 * HD), j)],
        out_specs=pl.BlockSpec((N, 128), lambda b, h: (b, h)),
        out_shape=jax.ShapeDtypeStruct((LAT, D), BF16),
        scratch_shapes=[
            pltpu.VMEM((CTX + N, 128), BF16),
            pltpu.VMEM((128 + DIFF_SUM_ROWS, CTX + N), BF16),
            pltpu.VMEM((2, CTX + N, TQ), F32),
            pltpu.VMEM((2, 8, TQ), F32),
        ],
        compiler_params=_cparams(("arbitrary", "arbitrary")),
        name="diff_attn",
    )(qkv, qkv, qkv, qkv, qkv, *lams, g)


def _diff_ctx(qkv, lams, g, j, lam_init):
    nh = D // 128
    r0 = LAT // CTX
    return pl.pallas_call(
        functools.partial(_diff_ctx_kernel, lam_init),
        grid=(B, nh),
        in_specs=[
            pl.BlockSpec((CTX, 128), lambda b, h: (r0 + b, h)),
            pl.BlockSpec((CTX, 128), lambda b, h: (r0 + b, nh + h)),
            pl.BlockSpec((CTX, 128), lambda b, h: (r0 + b, 2 * nh + h)),
        ] + [_layer_spec((1, HD), j)] * 4 + [_layer_spec((1, 2 * HD), j)],
        out_specs=pl.BlockSpec((CTX, 128), lambda b, h: (b, h)),
        out_shape=jax.ShapeDtypeStruct((B * CTX, D), BF16),
        compiler_params=_cparams(("arbitrary", "arbitrary")),
        name="diff_attn_ctx",
    )(qkv, qkv, qkv, *lams, g)


def _post_kernel(last, n_x, widths, *refs):
    n_att = len(widths) * (1 if last else 2)
    x_refs = refs[:n_x]
    att_refs = refs[n_x:n_x + n_att]
    mod_ref, g_ref, wo_ref, wi_ref, w2_ref, fg_ref, o_ref = refs[n_x + n_att:]
    x = _load_rows(x_refs)
    is_latent = pl.program_id(0) < NT_LAT
    y = None
    off = 0
    for i, w in enumerate(widths):
        if last:
            a = att_refs[i][...]
        else:
            a = jnp.where(is_latent, att_refs[2 * i][...], att_refs[2 * i + 1][...])
        d = _dot(a, wo_ref[off:off + w, :])
        y = d if y is None else y + d
        off += w
    x2 = x + mod_ref[2:3, :] * y
    h = _norm_mod(x2, g_ref[...], mod_ref[3:4, :], mod_ref[4:5, :]).astype(BF16)
    acc = jnp.zeros((TM, D), F32)
    for c in range(MLP_H // D):
        u = jnp.maximum(_dot(h, wi_ref[:, c * D:(c + 1) * D]), 0.0)
        acc = acc + _dot((u * u).astype(BF16), w2_ref[c * D:(c + 1) * D, :])
    x3 = x2 + mod_ref[5:6, :] * acc
    if last:
        ms = jnp.mean(x3 * x3, axis=-1, keepdims=True)
        x3 = x3 * lax.rsqrt(ms + NORM_EPS) * fg_ref[...]
    o_ref[...] = x3


def _post(xs, att, mods, i, g_all, wo_all, j, wi_all, w2_all, fg, last):
    nt = NT_LAT if last else NT
    x_specs, x_args = _x_specs(xs)
    widths = tuple(a_lat.shape[1] for a_lat, _ in att)
    assert sum(widths) == D
    att_specs, att_args = [], []
    for (a_lat, a_ctx), w in zip(att, widths):
        att_specs.append(pl.BlockSpec((TM, w), lambda t: (jnp.minimum(t, NT_LAT - 1), 0)))
        att_args.append(a_lat)
        if not last:
            att_specs.append(pl.BlockSpec((TM, w), lambda t: (jnp.maximum(t - NT_LAT, 0), 0)))
            att_args.append(a_ctx)
    once = dict(pipeline_mode=pl.Buffered(1))
    return pl.pallas_call(
        functools.partial(_post_kernel, last, len(x_args), widths),
        grid=(nt,),
        in_specs=x_specs + att_specs + [
            _mod_spec(i),
            _layer_spec((1, D), i),
            _layer_spec((D, D), j, **once),
            _layer_spec((D, MLP_H), i, **once),
            _layer_spec((MLP_H, D), i, **once),
            pl.BlockSpec((1, D), lambda t: (0, 0)),
        ],
        out_specs=pl.BlockSpec((TM, D), lambda t: (t, 0)),
        out_shape=jax.ShapeDtypeStruct((nt * TM, D), F32),
        compiler_params=_cparams(("arbitrary",)),
        name="post_last" if last else "post",
    )(*x_args, *att_args, mods, g_all, wo_all, wi_all, w2_all, fg)


def _channel_dft():
    n = np.arange(FD)
    ang = 2.0 * np.pi * ((n[:, None] * n[None, :]) % FD) / FD
    c = np.cos(ang) / math.sqrt(FD)
    s = np.sin(ang) / math.sqrt(FD)
    m = np.zeros((FW, 2 * FW), np.float32)
    for g in range(FG):
        m[g * FD:(g + 1) * FD, g * FD:(g + 1) * FD] = c
        m[g * FD:(g + 1) * FD, FW + g * FD:FW + (g + 1) * FD] = s
    return jnp.asarray(m).astype(BF16)


def _ctx_dft():
    n = np.arange(CTX)
    ang = 2.0 * np.pi * ((n[:, None] * n[None, :]) % CTX) / CTX
    m = np.concatenate([np.cos(ang), np.sin(ang)], axis=1) / math.sqrt(CTX)
    return jnp.asarray(m.astype(np.float32)).astype(BF16)


def _position_dft_tables():
    half = N // 2
    l = jnp.arange(half, dtype=jnp.int32)[None, :]
    dk = jnp.arange(TMF, dtype=jnp.int32)[:, None]
    ang = ((dk * l) % N).astype(F32) * (2.0 * math.pi / N)
    k0 = (jnp.arange(N // TMF, dtype=jnp.int32) * TMF)[:, None]
    ang0 = ((k0 * l) % N).astype(F32) * (2.0 * math.pi / N)
    scale = 1.0 / math.sqrt(N)
    ca = (jnp.cos(ang0) * scale).reshape(N // TMF, 1, half)
    sa = (jnp.sin(ang0) * scale).reshape(N // TMF, 1, half)
    return jnp.cos(ang), jnp.sin(ang), ca, sa


def _rope_tables():
    t = jnp.arange(N)
    rows, cols = t // GW, t % GW
    nf = HD // 4
    inv_freq = ROPE_THETA ** (-jnp.arange(nf, dtype=F32) / nf)
    lane = np.arange(128)
    d = lane % HD
    use_col = d >= HD // 2
    first = (d % (HD // 2)) < nf
    pos = jnp.where(use_col[None, :], cols[:, None], rows[:, None]).astype(F32)
    ang = pos * inv_freq[d % nf][None, :]
    cos, sin = jnp.cos(ang), jnp.sin(ang)
    sa = jnp.where(first[None, :], -sin, 0.0)
    sb = jnp.where(first[None, :], 0.0, sin)
    pad = jnp.zeros((TM, 128), F32)
    return (jnp.concatenate([cos, pad + 1.0]), jnp.concatenate([sa, pad]), jnp.concatenate([sb, pad]))


def kernel(x, c, ctx, c_ctx, w_mod, b_mod, norm_mix_g, norm_mlp_g, w_mlp_in, w_mlp_out, w_in_ab, w_out_ab,
           na_rpb, w_qkv_diff, w_out_diff, diff_lq1, diff_lk1, diff_lq2, diff_lk2, diff_subln_g,
           final_norm_g):
    assert x.shape == (B, N, D) and ctx.shape == (B, CTX, D)
    xs = (x.reshape(LAT, D), ctx.reshape(B * CTX, D))
    cin = jnp.concatenate([c, c_ctx[None, :], jnp.zeros((8 - B - 1, D), F32)], axis=0)
    mods = _modulation(cin, w_mod, b_mod).reshape(DEPTH, 8, 6, D)

    cs = _channel_dft()
    mctx = _ctx_dft()
    tc, ts, ca, sa = _position_dft_tables()
    rope = _rope_tables()
    bias = _na_bias(na_rpb.reshape(-1))
    fg = final_norm_g.reshape(1, D)

    g_mix = norm_mix_g.reshape(DEPTH, 1, D)
    g_mlp = norm_mlp_g.reshape(DEPTH, 1, D)
    w_ab, wo_ab = w_in_ab.astype(BF16), w_out_ab.astype(BF16)
    w_qkv, wo_diff = w_qkv_diff.astype(BF16), w_out_diff.astype(BF16)
    w_in, w_out = w_mlp_in.astype(BF16), w_mlp_out.astype(BF16)
    lams = [v.reshape(-1, 1, HD) for v in (diff_lq1, diff_lk1, diff_lq2, diff_lk2)]
    g_sub = diff_subln_g.reshape(-1, 1, 2 * HD)

    for i in range(DEPTH):
        last = i == DEPTH - 1
        j = i // 2
        if i % 2 == 0:
            fcs, qkv = _proj_even(xs, mods, i, g_mix, w_ab, j, cs)
            att = [(_fourier(fcs, tc, ts, ca, sa), None if last else _fourier_ctx(fcs, mctx)),
                   (_na(qkv, bias, j), None if last else _na_ctx(qkv))]
            wo = wo_ab
        else:
            lam_init = 0.8 - 0.6 * math.exp(-0.3 * i)
            qkv = _proj_odd(xs, mods, i, g_mix, w_qkv, j, *rope)
            att = [(_diff(qkv, lams, g_sub, j, lam_init),
                    None if last else _diff_ctx(qkv, lams, g_sub, j, lam_init))]
            wo = wo_diff
        xs = _post(xs, att, mods, i, g_mlp, wo, j, w_in, w_out, fg, last)
    return xs.reshape(B, N, D)
```

```python
import functools
import math

import numpy as np
import jax
import jax.numpy as jnp
from jax import lax
from jax.experimental import pallas as pl
from jax.experimental.pallas import tpu as pltpu

F32 = jnp.float32
BF16 = jnp.bfloat16

D = 1024
B = 4
N = 4096
CTX = 256
DEPTH = 4
GW = 64
GH = N // GW
FW = 512
FG = 4
FD = 128
NAW = 512
HD = 64
NA_KH = 8
NA_KW = 16
MLP_H = 4 * D
NORM_EPS = 1e-6
SUBLN_EPS = 1e-5
NEG = -1e30
ROPE_THETA = 10000.0
LOG2E = 1.4426950408889634
Q_SCALE = HD ** -0.5 * LOG2E

TM = 512
LAT = B * N
ROWS = LAT + B * CTX
NT = ROWS // TM
NT_LAT = LAT // TM
TILES_PER_BATCH = N // TM
TQ = 512
QROWS = TQ // GW
NA_WIN_ROWS = 16
NA_WIN = NA_WIN_ROWS * GW
NA_CK = 256
TMF = 512
REV_BLK = 256
DIFF_SUM_ROWS = 16
DIFF_CHUNKS = tuple(range(0, CTX + N + 1, 256))

VMEM_LIMIT = 60 * 1024 * 1024


def _cparams(sem):
    return pltpu.CompilerParams(dimension_semantics=sem, vmem_limit_bytes=VMEM_LIMIT)


def _mod_kernel(cin_ref, w_ref, b_ref, o_ref):
    cin = cin_ref[...]
    act = cin * jax.nn.sigmoid(cin)
    o_ref[...] = jnp.dot(act, w_ref[...], preferred_element_type=F32,
                         precision=lax.Precision.HIGHEST) + b_ref[...]


def _modulation(cin, w_mod, b_mod):
    tn = 3072
    return pl.pallas_call(
        _mod_kernel,
        grid=(DEPTH, 6 * D // tn),
        in_specs=[
            pl.BlockSpec((8, D), lambda l, j: (0, 0)),
            pl.BlockSpec((None, D, tn), lambda l, j: (l, 0, j)),
            pl.BlockSpec((None, 1, tn), lambda l, j: (l, 0, j)),
        ],
        out_specs=pl.BlockSpec((None, 8, tn), lambda l, j: (l, 0, j)),
        out_shape=jax.ShapeDtypeStruct((DEPTH, 8, 6 * D), F32),
        compiler_params=_cparams(("arbitrary", "arbitrary")),
        name="modulation",
    )(cin, w_mod, b_mod.reshape(DEPTH, 1, 6 * D))


def _norm_mod(x, g, shift, scale):
    ms = jnp.mean(x * x, axis=-1, keepdims=True)
    y = x * lax.rsqrt(ms + NORM_EPS) * g
    return y * (1.0 + scale) + shift


def _dot(a, b):
    return jnp.dot(a, b, preferred_element_type=F32)


def _dot_t(a, b):
    return lax.dot_general(a, b, (((1,), (1,)), ((), ())), preferred_element_type=F32)


def _softmax_pv(s, v_ext):
    m = jnp.max(s, axis=1, keepdims=True)
    p = jnp.exp2((s - m).astype(BF16))
    return _dot(p, v_ext)


def _lane_lo():
    return lax.broadcasted_iota(jnp.int32, (1, 2 * HD), 1) < HD


def _x_specs(xs):
    if isinstance(xs, tuple):
        return ([pl.BlockSpec((TM, D), lambda t: (jnp.minimum(t, NT_LAT - 1), 0)),
                 pl.BlockSpec((TM, D), lambda t: (jnp.maximum(t - NT_LAT, 0), 0))], list(xs))
    return [pl.BlockSpec((TM, D), lambda t: (t, 0))], [xs]


def _load_rows(refs):
    if len(refs) == 1:
        return refs[0][...]
    return jnp.where(pl.program_id(0) < NT_LAT, refs[0][...], refs[1][...])


def _layer_spec(shape, layer, **kw):
    return pl.BlockSpec((None,) + shape, lambda *_: (layer,) + (0,) * len(shape), **kw)


def _mod_spec(layer):
    return pl.BlockSpec((None, None, 6, D), lambda t: (layer, t // TILES_PER_BATCH, 0, 0))


def _proj_even_kernel(n_x, *refs):
    x_refs = refs[:n_x]
    mod_ref, g_ref, w_ref, cs_ref, fcs_ref, qkv_ref = refs[n_x:]
    h = _norm_mod(_load_rows(x_refs), g_ref[...], mod_ref[0:1, :], mod_ref[1:2, :]).astype(BF16)
    p = _dot(h, w_ref[...])
    f = p[:, :FW].astype(BF16)
    fcs_ref[...] = _dot(f, cs_ref[...]).astype(BF16)
    qkv_ref[:, :NAW] = (p[:, FW:FW + NAW] * Q_SCALE).astype(BF16)
    qkv_ref[:, NAW:] = p[:, FW + NAW:].astype(BF16)


def _proj_even(xs, mods, i, g_all, w_all, j, cs):
    nout = FW + 3 * NAW
    x_specs, x_args = _x_specs(xs)
    return pl.pallas_call(
        functools.partial(_proj_even_kernel, len(x_args)),
        grid=(NT,),
        in_specs=x_specs + [
            _mod_spec(i),
            _layer_spec((1, D), i),
            _layer_spec((D, nout), j),
            pl.BlockSpec((FW, 2 * FW), lambda t: (0, 0)),
        ],
        out_specs=[
            pl.BlockSpec((TM, 2 * FW), lambda t: (t, 0)),
            pl.BlockSpec((TM, 3 * NAW), lambda t: (t, 0)),
        ],
        out_shape=[
            jax.ShapeDtypeStruct((ROWS, 2 * FW), BF16),
            jax.ShapeDtypeStruct((ROWS, 3 * NAW), BF16),
        ],
        compiler_params=_cparams(("arbitrary",)),
        name="proj_even",
    )(*x_args, mods, g_all, w_all, cs)


def _proj_odd_kernel(x_ref, mod_ref, g_ref, w_ref, cos_ref, sa_ref, sb_ref, qkv_ref):
    h = _norm_mod(x_ref[...], g_ref[...], mod_ref[0:1, :], mod_ref[1:2, :]).astype(BF16)
    p = _dot(h, w_ref[...])
    cos = cos_ref[...]
    sa = sa_ref[...]
    sb = sb_ref[...]
    for c in range(2 * D // 128):
        xc = p[:, c * 128:(c + 1) * 128]
        r = xc * cos + pltpu.roll(xc, 112, 1) * sa + pltpu.roll(xc, 16, 1) * sb
        if c < D // 128:
            r = r * Q_SCALE
        qkv_ref[:, c * 128:(c + 1) * 128] = r.astype(BF16)
    qkv_ref[:, 2 * D:] = p[:, 2 * D:].astype(BF16)


def _proj_odd(xs, mods, i, g_all, w_all, j, cos, sa, sb):
    nout = 3 * D

    def rope_idx(t):
        return (jnp.where(t < NT_LAT, t % TILES_PER_BATCH, TILES_PER_BATCH), 0)

    return pl.pallas_call(
        _proj_odd_kernel,
        grid=(NT,),
        in_specs=[
            pl.BlockSpec((TM, D), lambda t: (t, 0)),
            _mod_spec(i),
            _layer_spec((1, D), i),
            _layer_spec((D, nout), j),
            pl.BlockSpec((TM, 128), rope_idx),
            pl.BlockSpec((TM, 128), rope_idx),
            pl.BlockSpec((TM, 128), rope_idx),
        ],
        out_specs=pl.BlockSpec((TM, nout), lambda t: (t, 0)),
        out_shape=jax.ShapeDtypeStruct((ROWS, nout), BF16),
        compiler_params=_cparams(("arbitrary",)),
        name="proj_odd",
    )(xs, mods, g_all, w_all, cos, sa, sb)


def _fourier_kernel(x_ref, rev_ref, tc_ref, ts_ref, ca_ref, sa_ref, o_ref, eo):
    half = N // 2
    rb = REV_BLK

    @pl.when(pl.program_id(1) == 0)
    def _():
        first = lax.broadcasted_iota(jnp.int32, (rb, FW), 0) == 0
        for l0 in range(0, half, rb):
            if l0 == 0:
                win = jnp.concatenate([x_ref[N - rb:N, :], x_ref[0:rb, :]], axis=0)
            else:
                win = x_ref[N - l0 - rb:N - l0 + rb, :]
            xr = _dot(rev_ref[...], win)
            xf = x_ref[l0:l0 + rb, :].astype(F32)
            e = xf[:, :FW] + xr[:, :FW]
            if l0 == 0:
                e = jnp.where(first, xf[:, :FW], e)
            eo[l0:l0 + rb, :FW] = e.astype(BF16)
            eo[l0:l0 + rb, FW:] = (xf[:, FW:] - xr[:, FW:]).astype(BF16)

    tc = tc_ref[...]
    ts = ts_ref[...]
    ca = ca_ref[...]
    sa = sa_ref[...]
    mc = (tc * ca - ts * sa).astype(BF16)
    ms = (ts * ca + tc * sa).astype(BF16)
    y = _dot(mc, eo[:, :FW]) - _dot(ms, eo[:, FW:])
    odd = (lax.broadcasted_iota(jnp.int32, (TMF, FW), 0) & 1) == 1
    mid = x_ref[half:half + 1, :FW].astype(F32) * (1.0 / math.sqrt(N))
    o_ref[...] = (y + jnp.where(odd, -mid, mid)).astype(BF16)


def _fourier(fcs, rev, tc, ts, ca, sa):
    nti = N // TMF
    half = N // 2
    return pl.pallas_call(
        _fourier_kernel,
        grid=(B, nti),
        in_specs=[
            pl.BlockSpec((N, 2 * FW), lambda b, i: (b, 0)),
            pl.BlockSpec((REV_BLK, 2 * REV_BLK), lambda b, i: (0, 0)),
            pl.BlockSpec((TMF, half), lambda b, i: (0, 0)),
            pl.BlockSpec((TMF, half), lambda b, i: (0, 0)),
            pl.BlockSpec((None, 1, half), lambda b, i: (i, 0, 0)),
            pl.BlockSpec((None, 1, half), lambda b, i: (i, 0, 0)),
        ],
        out_specs=pl.BlockSpec((TMF, FW), lambda b, i: (b * nti + i, 0)),
        out_shape=jax.ShapeDtypeStruct((LAT, FW), BF16),
        scratch_shapes=[pltpu.VMEM((half, 2 * FW), BF16)],
        compiler_params=_cparams(("arbitrary", "arbitrary")),
        name="fourier",
    )(fcs, rev, tc, ts, ca, sa)


def _fourier_ctx_kernel(x_ref, m_ref, o_ref):
    y = _dot(m_ref[:, :CTX], x_ref[:, :FW]) - _dot(m_ref[:, CTX:], x_ref[:, FW:])
    o_ref[...] = y.astype(BF16)


def _fourier_ctx(fcs, mctx):
    return pl.pallas_call(
        _fourier_ctx_kernel,
        grid=(B,),
        in_specs=[
            pl.BlockSpec((CTX, 2 * FW), lambda b: (LAT // CTX + b, 0)),
            pl.BlockSpec((CTX, 2 * CTX), lambda b: (0, 0)),
        ],
        out_specs=pl.BlockSpec((CTX, FW), lambda b: (b, 0)),
        out_shape=jax.ShapeDtypeStruct((B * CTX, FW), BF16),
        compiler_params=_cparams(("arbitrary",)),
        name="fourier_ctx",
    )(fcs, mctx)


def _na_bias_kernel(rpb_ref, o_ref):
    j = pl.program_id(0)
    p = pl.program_id(1)
    cq = lax.broadcasted_iota(jnp.int32, (GW, 2 * GW), 0)
    lane = lax.broadcasted_iota(jnp.int32, (GW, 2 * GW), 1)
    ck = lane & (GW - 1)
    left = lane < GW
    cs = jnp.clip(cq - NA_KW // 2, 0, GW - NA_KW)
    col_ok = (ck >= cs) & (ck < cs + NA_KW)
    bidx = ck - cq + (NA_KW - 1)
    zero = jnp.zeros((GW, 2 * GW), F32)
    neg = jnp.full((GW, 2 * GW), NEG, F32)
    neg_l = jnp.where(left, neg, zero)
    neg_r = jnp.where(left, zero, neg)
    nb = 2 * NA_KW - 1
    na = 2 * NA_KH - 1
    for hh in range(2):
        base = ((j * 8 + 2 * p + hh) * na) * nb
        halves_l, halves_r = [], []
        for a in range(na):
            r = zero
            for b in range(nb):
                r = jnp.where(bidx == b, rpb_ref[base + a * nb + b], r)
            r = jnp.where(col_ok, r * LOG2E, neg)
            halves_l.append(jnp.where(left, r, zero))
            halves_r.append(jnp.where(left, zero, r))
        for cls, r0 in enumerate((0, QROWS, GH - QROWS)):
            start = min(max(r0 - NA_KH // 2, 0), GH - NA_WIN_ROWS)
            for rql in range(QROWS):
                rq = r0 + rql
                rs = min(max(rq - NA_KH // 2, 0), GH - NA_KH)
                for m in range(NA_WIN_ROWS // 2):
                    rk0 = start + 2 * m
                    rk1 = rk0 + 1
                    tl = halves_l[rk0 - rq + NA_KH - 1] if rs <= rk0 < rs + NA_KH else neg_l
                    tr = halves_r[rk1 - rq + NA_KH - 1] if rs <= rk1 < rs + NA_KH else neg_r
                    o_ref[cls, hh, rql * GW:(rql + 1) * GW, m * 128:(m + 1) * 128] = (tl + tr).astype(BF16)


def _na_bias(rpb_flat):
    n_even = (DEPTH + 1) // 2
    return pl.pallas_call(
        _na_bias_kernel,
        grid=(n_even, 4),
        in_specs=[pl.BlockSpec(memory_space=pltpu.SMEM)],
        out_specs=pl.BlockSpec((None, None, 3, 2, TQ, NA_WIN), lambda j, p: (j, p, 0, 0, 0, 0)),
        out_shape=jax.ShapeDtypeStruct((n_even, 4, 3, 2, TQ, NA_WIN), BF16),
        compiler_params=_cparams(("arbitrary", "arbitrary")),
        name="na_bias",
    )(rpb_flat)


def _na_kernel(q_ref, k_ref, v_ref, kc_ref, vc_ref, bias_ref, o_ref, s_buf, m_buf):
    nq = N // TQ
    nwc = NA_WIN // NA_CK
    lo = _lane_lo()
    kc = kc_ref[...]
    vc = vc_ref[...]

    def win_start(t):
        return pl.multiple_of(jnp.clip(t * QROWS - NA_KH // 2, 0, GH - NA_WIN_ROWS) * GW, 256)

    def stage(qk_t, qk_h, pv_t, pv_h):
        if qk_h is not None:
            q = q_ref[pl.ds(pl.multiple_of(qk_t * TQ, TQ), TQ), :]
            qh = jnp.where(lo if qk_h == 0 else jnp.logical_not(lo), q, jnp.zeros_like(q))
            k0 = win_start(qk_t)
            cls = jnp.where(qk_t == 0, 0, jnp.where(qk_t == nq - 1, 2, 1))
            m128 = None
        if pv_h is not None:
            m = jnp.max(m_buf[pv_h], axis=1, keepdims=True)
            v0 = win_start(pv_t)
            own = lo if pv_h == 0 else jnp.logical_not(lo)
            acc = None
        for c in range(nwc + 1):
            cols = slice(c * NA_CK, (c + 1) * NA_CK)
            if qk_h is not None:
                if c < nwc:
                    s = _dot_t(qh, k_ref[pl.ds(k0 + c * NA_CK, NA_CK), :])
                    s = s + bias_ref[cls, qk_h, :, cols].astype(F32)
                else:
                    s = _dot_t(qh, kc)
                s_buf[qk_h, :, cols] = s
                cm = jnp.maximum(s[:, :128], s[:, 128:])
                m128 = cm if m128 is None else jnp.maximum(m128, cm)
            if pv_h is not None:
                p = jnp.exp2((s_buf[pv_h, :, cols] - m).astype(BF16))
                vb = v_ref[pl.ds(v0 + c * NA_CK, NA_CK), :] if c < nwc else vc
                d = _dot(p, jnp.where(own, vb, jnp.ones_like(vb)))
                acc = d if acc is None else acc + d
        if qk_h is not None:
            m_buf[qk_h] = m128
        if pv_h is not None:
            return acc / pltpu.roll(acc, HD, 1)
        return None

    stage(0, 0, None, None)

    def tile_body(t, carry):
        o0 = stage(t, 1, t, 0)
        o1 = stage(jnp.minimum(t + 1, nq - 1), 0, t, 1)
        o_ref[pl.ds(pl.multiple_of(t * TQ, TQ), TQ), :] = jnp.where(lo, o0, o1).astype(BF16)
        return carry

    lax.fori_loop(0, nq, tile_body, 0)


def _na(qkv, bias, j):
    return pl.pallas_call(
        _na_kernel,
        grid=(4, B),
        in_specs=[
            pl.BlockSpec((N, 128), lambda p, b: (b, p)),
            pl.BlockSpec((N, 128), lambda p, b: (b, 4 + p)),
            pl.BlockSpec((N, 128), lambda p, b: (b, 8 + p)),
            pl.BlockSpec((CTX, 128), lambda p, b: (LAT // CTX + b, 4 + p)),
            pl.BlockSpec((CTX, 128), lambda p, b: (LAT // CTX + b, 8 + p)),
            pl.BlockSpec((None, None, 3, 2, TQ, NA_WIN), lambda p, b: (j, p, 0, 0, 0, 0)),
        ],
        out_specs=pl.BlockSpec((N, 128), lambda p, b: (b, p)),
        out_shape=jax.ShapeDtypeStruct((LAT, NAW), BF16),
        scratch_shapes=[
            pltpu.VMEM((2, TQ, NA_WIN + CTX), F32),
            pltpu.VMEM((2, TQ, 128), F32),
        ],
        compiler_params=_cparams(("arbitrary", "arbitrary")),
        name="na",
    )(qkv, qkv, qkv, qkv, qkv, bias)


def _na_ctx_kernel(q_ref, kc_ref, vc_ref, o_ref):
    q = q_ref[...]
    kc = kc_ref[...]
    vc = vc_ref[...]
    lo = _lane_lo()
    outs = []
    for hh in range(2):
        own = lo if hh == 0 else jnp.logical_not(lo)
        qh = jnp.where(own, q, jnp.zeros_like(q))
        o = _softmax_pv(_dot_t(qh, kc), jnp.where(own, vc, jnp.ones_like(vc)))
        outs.append(o / pltpu.roll(o, HD, 1))
    o_ref[...] = jnp.where(lo, outs[0], outs[1]).astype(BF16)


def _na_ctx(qkv):
    r0 = LAT // CTX
    return pl.pallas_call(
        _na_ctx_kernel,
        grid=(B, 4),
        in_specs=[
            pl.BlockSpec((CTX, 128), lambda b, p: (r0 + b, p)),
            pl.BlockSpec((CTX, 128), lambda b, p: (r0 + b, 4 + p)),
            pl.BlockSpec((CTX, 128), lambda b, p: (r0 + b, 8 + p)),
        ],
        out_specs=pl.BlockSpec((CTX, 128), lambda b, p: (b, p)),
        out_shape=jax.ShapeDtypeStruct((B * CTX, NAW), BF16),
        compiler_params=_cparams(("arbitrary", "arbitrary")),
        name="na_ctx",
    )(qkv, qkv, qkv)


def _diff_lambda(lq1_ref, lk1_ref, lq2_ref, lk2_ref, lam_init):
    s1 = jnp.sum(lq1_ref[...] * lk1_ref[...], axis=1, keepdims=True)
    s2 = jnp.sum(lq2_ref[...] * lk2_ref[...], axis=1, keepdims=True)
    return jnp.exp(s1) - jnp.exp(s2) + lam_init


def _diff_finish(o1, o2, lam, g, lam_init):
    o = o1 - lam * o2
    ms = jnp.mean(o * o, axis=-1, keepdims=True)
    return (o * lax.rsqrt(ms + SUBLN_EPS) * g * (1.0 - lam_init)).astype(BF16)


def _diff_kernel(lam_init, q_ref, kl_ref, vl_ref, kc_ref, vc_ref, lq1_ref, lk1_ref, lq2_ref, lk2_ref,
                 g_ref, o_ref, k_all, vt_ext, s_buf, m_buf):
    k_all[0:CTX, :] = kc_ref[...]
    k_all[CTX:, :] = kl_ref[...]
    vt_ext[0:128, 0:CTX] = vc_ref[...].astype(F32).T.astype(BF16)
    vt_ext[0:128, CTX:] = vl_ref[...].astype(F32).T.astype(BF16)
    vt_ext[128:, :] = jnp.ones((DIFF_SUM_ROWS, CTX + N), BF16)

    lam = _diff_lambda(lq1_ref, lk1_ref, lq2_ref, lk2_ref, lam_init)
    g_out = g_ref[...] * (1.0 - lam_init)
    lo = _lane_lo()
    nq = N // TQ

    def stage(qk_tile, qk_slot, pv_slot):
        if qk_slot is not None:
            q = q_ref[pl.ds(pl.multiple_of(qk_tile * TQ, TQ), TQ), :]
            qj = jnp.where(lo if qk_slot == 0 else jnp.logical_not(lo), q, jnp.zeros_like(q))
            m8 = None
        if pv_slot is not None:
            m = jnp.max(m_buf[pv_slot], axis=0, keepdims=True)
            acc = None

        def numerators(c, m_c):
            rows = slice(DIFF_CHUNKS[c], DIFF_CHUNKS[c + 1])
            return jnp.exp2((s_buf[pv_slot, rows, :] - m_c).astype(BF16))

        nck = len(DIFF_CHUNKS) - 1
        p_next = numerators(0, m) if pv_slot is not None else None
        cm_prev = None
        for c in range(nck):
            rows = slice(DIFF_CHUNKS[c], DIFF_CHUNKS[c + 1])
            m_c = m if pv_slot is not None else None
            if qk_slot is not None:
                if pv_slot is not None and cm_prev is not None:
                    m_c = jnp.maximum(m, jnp.minimum(cm_prev[0:1, :], m))
                s = _dot_t(k_all[rows, :], qj)
                s_buf[qk_slot, rows, :] = s
                cm_prev = jnp.max(s.reshape(-1, 8, TQ), axis=0)
                m8 = cm_prev if m8 is None else jnp.maximum(m8, cm_prev)
            if pv_slot is not None:
                p = p_next
                if c + 1 < nck:
                    p_next = numerators(c + 1, m_c)
                d = _dot(vt_ext[:, rows], p)
                acc = d if acc is None else acc + d
        if qk_slot is not None:
            m_buf[qk_slot] = m8
        if pv_slot is not None:
            return acc[0:128, :] / acc[128:129, :]
        return None

    stage(0, 0, None)

    def tile_body(t, carry):
        o1 = stage(t, 1, 0)
        o2 = stage(jnp.minimum(t + 1, nq - 1), 0, 1)
        o = o1 - lam * o2
        ms = jnp.mean(o * o, axis=0, keepdims=True)
        on = (o * lax.rsqrt(ms + SUBLN_EPS)).T
        o_ref[pl.ds(pl.multiple_of(t * TQ, TQ), TQ), :] = (on * g_out).astype(BF16)
        return carry

    lax.fori_loop(0, nq, tile_body, 0)


def _diff_ctx_kernel(lam_init, q_ref, kc_ref, vc_ref, lq1_ref, lk1_ref, lq2_ref, lk2_ref, g_ref, o_ref):
    lam = _diff_lambda(lq1_ref, lk1_ref, lq2_ref, lk2_ref, lam_init)
    q = q_ref[...]
    lo = _lane_lo()
    outs = []
    vc = vc_ref[...]
    v_ext = jnp.concatenate([vc, jnp.ones_like(vc)], axis=1)
    for jj in range(2):
        qj = jnp.where(lo if jj == 0 else jnp.logical_not(lo), q, jnp.zeros_like(q))
        o = _softmax_pv(_dot_t(qj, kc_ref[...]), v_ext)
        outs.append(o[:, :128] / o[:, 128:])
    o_ref[...] = _diff_finish(outs[0], outs[1], lam, g_ref[...], lam_init)


def _diff(qkv, lams, g, j, lam_init):
    nh = D // 128
    r0 = LAT // CTX
    return pl.pallas_call(
        functools.partial(_diff_kernel, lam_init),
        grid=(B, nh),
        in_specs=[
            pl.BlockSpec((N, 128), lambda b, h: (b, h)),
            pl.BlockSpec((N, 128), lambda b, h: (b, nh + h)),
            pl.BlockSpec((N, 128), lambda b, h: (b, 2 * nh + h)),
            pl.BlockSpec((CTX, 128), lambda b, h: (r0 + b, nh + h)),
            pl.BlockSpec((CTX, 128), lambda b, h: (r0 + b, 2 * nh + h)),
        ] + [_layer_spec((1, HD), j)] * 4 + [_layer_spec((1, 2 * HD), j)],
        out_specs=pl.BlockSpec((N, 128), lambda b, h: (b, h)),
        out_shape=jax.ShapeDtypeStruct((LAT, D), BF16),
        scratch_shapes=[
            pltpu.VMEM((CTX + N, 128), BF16),
            pltpu.VMEM((128 + DIFF_SUM_ROWS, CTX + N), BF16),
            pltpu.VMEM((2, CTX + N, TQ), F32),
            pltpu.VMEM((2, 8, TQ), F32),
        ],
        compiler_params=_cparams(("arbitrary", "arbitrary")),
        name="diff_attn",
    )(qkv, qkv, qkv, qkv, qkv, *lams, g)


def _diff_ctx(qkv, lams, g, j, lam_init):
    nh = D // 128
    r0 = LAT // CTX
    return pl.pallas_call(
        functools.partial(_diff_ctx_kernel, lam_init),
        grid=(B, nh),
        in_specs=[
            pl.BlockSpec((CTX, 128), lambda b, h: (r0 + b, h)),
            pl.BlockSpec((CTX, 128), lambda b, h: (r0 + b, nh + h)),
            pl.BlockSpec((CTX, 128), lambda b, h: (r0 + b, 2 * nh + h)),
        ] + [_layer_spec((1, HD), j)] * 4 + [_layer_spec((1, 2 * HD), j)],
        out_specs=pl.BlockSpec((CTX, 128), lambda b, h: (b, h)),
        out_shape=jax.ShapeDtypeStruct((B * CTX, D), BF16),
        compiler_params=_cparams(("arbitrary", "arbitrary")),
        name="diff_attn_ctx",
    )(qkv, qkv, qkv, *lams, g)


def _post_kernel(last, n_x, widths, *refs):
    n_att = len(widths) * (1 if last else 2)
    x_refs = refs[:n_x]
    att_refs = refs[n_x:n_x + n_att]
    mod_ref, g_ref, wo_ref, wi_ref, w2_ref, fg_ref, o_ref = refs[n_x + n_att:]
    x = _load_rows(x_refs)
    is_latent = pl.program_id(0) < NT_LAT
    y = None
    off = 0
    for i, w in enumerate(widths):
        if last:
            a = att_refs[i][...]
        else:
            a = jnp.where(is_latent, att_refs[2 * i][...], att_refs[2 * i + 1][...])
        d = _dot(a, wo_ref[off:off + w, :])
        y = d if y is None else y + d
        off += w
    x2 = x + mod_ref[2:3, :] * y
    h = _norm_mod(x2, g_ref[...], mod_ref[3:4, :], mod_ref[4:5, :]).astype(BF16)
    acc = jnp.zeros((TM, D), F32)
    for c in range(MLP_H // D):
        u = jnp.maximum(_dot(h, wi_ref[:, c * D:(c + 1) * D]), 0.0)
        acc = acc + _dot((u * u).astype(BF16), w2_ref[c * D:(c + 1) * D, :])
    x3 = x2 + mod_ref[5:6, :] * acc
    if last:
        ms = jnp.mean(x3 * x3, axis=-1, keepdims=True)
        x3 = x3 * lax.rsqrt(ms + NORM_EPS) * fg_ref[...]
    o_ref[...] = x3


def _post(xs, att, mods, i, g_all, wo_all, j, wi_all, w2_all, fg, last):
    nt = NT_LAT if last else NT
    x_specs, x_args = _x_specs(xs)
    widths = tuple(a_lat.shape[1] for a_lat, _ in att)
    assert sum(widths) == D
    att_specs, att_args = [], []
    for (a_lat, a_ctx), w in zip(att, widths):
        att_specs.append(pl.BlockSpec((TM, w), lambda t: (jnp.minimum(t, NT_LAT - 1), 0)))
        att_args.append(a_lat)
        if not last:
            att_specs.append(pl.BlockSpec((TM, w), lambda t: (jnp.maximum(t - NT_LAT, 0), 0)))
            att_args.append(a_ctx)
    once = dict(pipeline_mode=pl.Buffered(1))
    return pl.pallas_call(
        functools.partial(_post_kernel, last, len(x_args), widths),
        grid=(nt,),
        in_specs=x_specs + att_specs + [
            _mod_spec(i),
            _layer_spec((1, D), i),
            _layer_spec((D, D), j, **once),
            _layer_spec((D, MLP_H), i, **once),
            _layer_spec((MLP_H, D), i, **once),
            pl.BlockSpec((1, D), lambda t: (0, 0)),
        ],
        out_specs=pl.BlockSpec((TM, D), lambda t: (t, 0)),
        out_shape=jax.ShapeDtypeStruct((nt * TM, D), F32),
        compiler_params=_cparams(("arbitrary",)),
        name="post_last" if last else "post",
    )(*x_args, *att_args, mods, g_all, wo_all, wi_all, w2_all, fg)


def _channel_dft():
    n = np.arange(FD)
    ang = 2.0 * np.pi * ((n[:, None] * n[None, :]) % FD) / FD
    c = np.cos(ang) / math.sqrt(FD)
    s = np.sin(ang) / math.sqrt(FD)
    m = np.zeros((FW, 2 * FW), np.float32)
    for g in range(FG):
        m[g * FD:(g + 1) * FD, g * FD:(g + 1) * FD] = c
        m[g * FD:(g + 1) * FD, FW + g * FD:FW + (g + 1) * FD] = s
    return jnp.asarray(m).astype(BF16)


def _ctx_dft():
    n = np.arange(CTX)
    ang = 2.0 * np.pi * ((n[:, None] * n[None, :]) % CTX) / CTX
    m = np.concatenate([np.cos(ang), np.sin(ang)], axis=1) / math.sqrt(CTX)
    return jnp.asarray(m.astype(np.float32)).astype(BF16)


def _row_reversal():
    m = np.zeros((REV_BLK, 2 * REV_BLK), np.float32)
    m[np.arange(REV_BLK), REV_BLK - np.arange(REV_BLK)] = 1.0
    return jnp.asarray(m).astype(BF16)


def _position_dft_tables():
    half = N // 2
    l = jnp.arange(half, dtype=jnp.int32)[None, :]
    dk = jnp.arange(TMF, dtype=jnp.int32)[:, None]
    ang = ((dk * l) % N).astype(F32) * (2.0 * math.pi / N)
    k0 = (jnp.arange(N // TMF, dtype=jnp.int32) * TMF)[:, None]
    ang0 = ((k0 * l) % N).astype(F32) * (2.0 * math.pi / N)
    scale = 1.0 / math.sqrt(N)
    ca = (jnp.cos(ang0) * scale).reshape(N // TMF, 1, half)
    sa = (jnp.sin(ang0) * scale).reshape(N // TMF, 1, half)
    return jnp.cos(ang), jnp.sin(ang), ca, sa


def _rope_tables():
    t = jnp.arange(N)
    rows, cols = t // GW, t % GW
    nf = HD // 4
    inv_freq = ROPE_THETA ** (-jnp.arange(nf, dtype=F32) / nf)
    lane = np.arange(128)
    d = lane % HD
    use_col = d >= HD // 2
    first = (d % (HD // 2)) < nf
    pos = jnp.where(use_col[None, :], cols[:, None], rows[:, None]).astype(F32)
    ang = pos * inv_freq[d % nf][None, :]
    cos, sin = jnp.cos(ang), jnp.sin(ang)
    sa = jnp.where(first[None, :], -sin, 0.0)
    sb = jnp.where(first[None, :], 0.0, sin)
    pad = jnp.zeros((TM, 128), F32)
    return (jnp.concatenate([cos, pad + 1.0]), jnp.concatenate([sa, pad]), jnp.concatenate([sb, pad]))


def kernel(x, c, ctx, c_ctx, w_mod, b_mod, norm_mix_g, norm_mlp_g, w_mlp_in, w_mlp_out, w_in_ab, w_out_ab,
           na_rpb, w_qkv_diff, w_out_diff, diff_lq1, diff_lk1, diff_lq2, diff_lk2, diff_subln_g,
           final_norm_g):
    assert x.shape == (B, N, D) and ctx.shape == (B, CTX, D)
    xs = (x.reshape(LAT, D), ctx.reshape(B * CTX, D))
    cin = jnp.concatenate([c, c_ctx[None, :], jnp.zeros((8 - B - 1, D), F32)], axis=0)
    mods = _modulation(cin, w_mod, b_mod).reshape(DEPTH, 8, 6, D)

    cs = _channel_dft()
    mctx = _ctx_dft()
    rev = _row_reversal()
    tc, ts, ca, sa = _position_dft_tables()
    rope = _rope_tables()
    bias = _na_bias(na_rpb.reshape(-1))
    fg = final_norm_g.reshape(1, D)

    g_mix = norm_mix_g.reshape(DEPTH, 1, D)
    g_mlp = norm_mlp_g.reshape(DEPTH, 1, D)
    w_ab, wo_ab = w_in_ab.astype(BF16), w_out_ab.astype(BF16)
    w_qkv, wo_diff = w_qkv_diff.astype(BF16), w_out_diff.astype(BF16)
    w_in, w_out = w_mlp_in.astype(BF16), w_mlp_out.astype(BF16)
    lams = [v.reshape(-1, 1, HD) for v in (diff_lq1, diff_lk1, diff_lq2, diff_lk2)]
    g_sub = diff_subln_g.reshape(-1, 1, 2 * HD)

    for i in range(DEPTH):
        last = i == DEPTH - 1
        j = i // 2
        if i % 2 == 0:
            fcs, qkv = _proj_even(xs, mods, i, g_mix, w_ab, j, cs)
            att = [(_fourier(fcs, rev, tc, ts, ca, sa), None if last else _fourier_ctx(fcs, mctx)),
                   (_na(qkv, bias, j), None if last else _na_ctx(qkv))]
            wo = wo_ab
        else:
            lam_init = 0.8 - 0.6 * math.exp(-0.3 * i)
            qkv = _proj_odd(xs, mods, i, g_mix, w_qkv, j, *rope)
            att = [(_diff(qkv, lams, g_sub, j, lam_init),
                    None if last else _diff_ctx(qkv, lams, g_sub, j, lam_init))]
            wo = wo_diff
        xs = _post(xs, att, mods, i, g_mlp, wo, j, w_in, w_out, fg, last)
    return xs.reshape(B, N, D)
```

```python
import functools
import math

import numpy as np
import jax
import jax.numpy as jnp
from jax import lax
from jax.experimental import pallas as pl
from jax.experimental.pallas import tpu as pltpu

F32 = jnp.float32
BF16 = jnp.bfloat16

D = 1024
B = 4
N = 4096
CTX = 256
DEPTH = 4
GW = 64
GH = N // GW
FW = 512
FG = 4
FD = 128
NAW = 512
HD = 64
NA_KH = 8
NA_KW = 16
MLP_H = 4 * D
NORM_EPS = 1e-6
SUBLN_EPS = 1e-5
NEG = -1e30
ROPE_THETA = 10000.0
LOG2E = 1.4426950408889634
Q_SCALE = HD ** -0.5 * LOG2E

TM = 512
LAT = B * N
ROWS = LAT + B * CTX
NT = ROWS // TM
NT_LAT = LAT // TM
TILES_PER_BATCH = N // TM
TQ = 512
QROWS = TQ // GW
NA_WIN_ROWS = 16
NA_WIN = NA_WIN_ROWS * GW
NA_CK = 256
TMF = 512
REV_BLK = 256
DIFF_SUM_ROWS = 16
DIFF_CHUNKS = tuple(range(0, CTX + N + 1, 256))

VMEM_LIMIT = 60 * 1024 * 1024


def _cparams(sem):
    return pltpu.CompilerParams(dimension_semantics=sem, vmem_limit_bytes=VMEM_LIMIT)


def _mod_kernel(cin_ref, w_ref, b_ref, o_ref):
    cin = cin_ref[...]
    act = cin * jax.nn.sigmoid(cin)
    a_hi = act.astype(BF16).astype(F32)
    a2 = jnp.concatenate([a_hi, act - a_hi], axis=0).astype(BF16)
    w = w_ref[...]
    w_hi = w.astype(BF16)
    w_lo = (w - w_hi.astype(F32)).astype(BF16)
    r = _dot(a2, w_hi)
    o_ref[...] = r[0:8, :] + r[8:16, :] + _dot(a_hi.astype(BF16), w_lo) + b_ref[...]


def _modulation(cin, w_mod, b_mod):
    tn = 3072
    return pl.pallas_call(
        _mod_kernel,
        grid=(DEPTH, 6 * D // tn),
        in_specs=[
            pl.BlockSpec((8, D), lambda l, j: (0, 0)),
            pl.BlockSpec((None, D, tn), lambda l, j: (l, 0, j)),
            pl.BlockSpec((None, 1, tn), lambda l, j: (l, 0, j)),
        ],
        out_specs=pl.BlockSpec((None, 8, tn), lambda l, j: (l, 0, j)),
        out_shape=jax.ShapeDtypeStruct((DEPTH, 8, 6 * D), F32),
        compiler_params=_cparams(("arbitrary", "arbitrary")),
        name="modulation",
    )(cin, w_mod, b_mod.reshape(DEPTH, 1, 6 * D))


def _norm_mod(x, g, shift, scale):
    ms = jnp.mean(x * x, axis=-1, keepdims=True)
    y = x * lax.rsqrt(ms + NORM_EPS) * g
    return y * (1.0 + scale) + shift


def _dot(a, b):
    return jnp.dot(a, b, preferred_element_type=F32)


def _dot_t(a, b):
    return lax.dot_general(a, b, (((1,), (1,)), ((), ())), preferred_element_type=F32)


def _softmax_pv(s, v_ext):
    m = jnp.max(s, axis=1, keepdims=True)
    p = jnp.exp2((s - m).astype(BF16))
    return _dot(p, v_ext)


def _lane_lo():
    return lax.broadcasted_iota(jnp.int32, (1, 2 * HD), 1) < HD


def _x_specs(xs):
    if isinstance(xs, tuple):
        return ([pl.BlockSpec((TM, D), lambda t: (jnp.minimum(t, NT_LAT - 1), 0)),
                 pl.BlockSpec((TM, D), lambda t: (jnp.maximum(t - NT_LAT, 0), 0))], list(xs))
    return [pl.BlockSpec((TM, D), lambda t: (t, 0))], [xs]


def _load_rows(refs):
    if len(refs) == 1:
        return refs[0][...]
    return jnp.where(pl.program_id(0) < NT_LAT, refs[0][...], refs[1][...])


def _layer_spec(shape, layer, **kw):
    return pl.BlockSpec((None,) + shape, lambda *_: (layer,) + (0,) * len(shape), **kw)


def _mod_spec(layer):
    return pl.BlockSpec((None, None, 6, D), lambda t: (layer, t // TILES_PER_BATCH, 0, 0))


def _proj_even_kernel(n_x, *refs):
    x_refs = refs[:n_x]
    mod_ref, g_ref, w_ref, cs_ref, fcs_ref, qkv_ref = refs[n_x:]
    h = _norm_mod(_load_rows(x_refs), g_ref[...], mod_ref[0:1, :], mod_ref[1:2, :]).astype(BF16)
    p = _dot(h, w_ref[...])
    f = p[:, :FW].astype(BF16)
    fcs_ref[...] = _dot(f, cs_ref[...]).astype(BF16)
    qkv_ref[:, :NAW] = (p[:, FW:FW + NAW] * Q_SCALE).astype(BF16)
    qkv_ref[:, NAW:] = p[:, FW + NAW:].astype(BF16)


def _proj_even(xs, mods, i, g_all, w_all, j, cs):
    nout = FW + 3 * NAW
    x_specs, x_args = _x_specs(xs)
    return pl.pallas_call(
        functools.partial(_proj_even_kernel, len(x_args)),
        grid=(NT,),
        in_specs=x_specs + [
            _mod_spec(i),
            _layer_spec((1, D), i),
            _layer_spec((D, nout), j),
            pl.BlockSpec((FW, 2 * FW), lambda t: (0, 0)),
        ],
        out_specs=[
            pl.BlockSpec((TM, 2 * FW), lambda t: (t, 0)),
            pl.BlockSpec((TM, 3 * NAW), lambda t: (t, 0)),
        ],
        out_shape=[
            jax.ShapeDtypeStruct((ROWS, 2 * FW), BF16),
            jax.ShapeDtypeStruct((ROWS, 3 * NAW), BF16),
        ],
        compiler_params=_cparams(("arbitrary",)),
        name="proj_even",
    )(*x_args, mods, g_all, w_all, cs)


def _proj_odd_kernel(x_ref, mod_ref, g_ref, w_ref, cos_ref, sa_ref, sb_ref, qkv_ref):
    h = _norm_mod(x_ref[...], g_ref[...], mod_ref[0:1, :], mod_ref[1:2, :]).astype(BF16)
    p = _dot(h, w_ref[...])
    cos = cos_ref[...]
    sa = sa_ref[...]
    sb = sb_ref[...]
    for c in range(2 * D // 128):
        xc = p[:, c * 128:(c + 1) * 128]
        r = xc * cos + pltpu.roll(xc, 112, 1) * sa + pltpu.roll(xc, 16, 1) * sb
        if c < D // 128:
            r = r * Q_SCALE
        qkv_ref[:, c * 128:(c + 1) * 128] = r.astype(BF16)
    qkv_ref[:, 2 * D:] = p[:, 2 * D:].astype(BF16)


def _proj_odd(xs, mods, i, g_all, w_all, j, cos, sa, sb):
    nout = 3 * D

    def rope_idx(t):
        return (jnp.where(t < NT_LAT, t % TILES_PER_BATCH, TILES_PER_BATCH), 0)

    return pl.pallas_call(
        _proj_odd_kernel,
        grid=(NT,),
        in_specs=[
            pl.BlockSpec((TM, D), lambda t: (t, 0)),
            _mod_spec(i),
            _layer_spec((1, D), i),
            _layer_spec((D, nout), j),
            pl.BlockSpec((TM, 128), rope_idx),
            pl.BlockSpec((TM, 128), rope_idx),
            pl.BlockSpec((TM, 128), rope_idx),
        ],
        out_specs=pl.BlockSpec((TM, nout), lambda t: (t, 0)),
        out_shape=jax.ShapeDtypeStruct((ROWS, nout), BF16),
        compiler_params=_cparams(("arbitrary",)),
        name="proj_odd",
    )(xs, mods, g_all, w_all, cos, sa, sb)


def _fourier_kernel(x_ref, rev_ref, tc_ref, ts_ref, ca_ref, sa_ref, o_ref, eo):
    half = N // 2
    rb = REV_BLK

    @pl.when(pl.program_id(1) == 0)
    def _():
        first = lax.broadcasted_iota(jnp.int32, (rb, FW), 0) == 0
        for l0 in range(0, half, rb):
            if l0 == 0:
                win = jnp.concatenate([x_ref[N - rb:N, :], x_ref[0:rb, :]], axis=0)
            else:
                win = x_ref[N - l0 - rb:N - l0 + rb, :]
            xr = _dot(rev_ref[...], win)
            xf = x_ref[l0:l0 + rb, :].astype(F32)
            e = xf[:, :FW] + xr[:, :FW]
            if l0 == 0:
                e = jnp.where(first, xf[:, :FW], e)
            eo[l0:l0 + rb, :FW] = e.astype(BF16)
            eo[l0:l0 + rb, FW:] = (xf[:, FW:] - xr[:, FW:]).astype(BF16)

    tc = tc_ref[...]
    ts = ts_ref[...]
    ca = ca_ref[...]
    sa = sa_ref[...]
    mc = (tc * ca - ts * sa).astype(BF16)
    ms = (ts * ca + tc * sa).astype(BF16)
    y = _dot(mc, eo[:, :FW]) - _dot(ms, eo[:, FW:])
    odd = (lax.broadcasted_iota(jnp.int32, (TMF, FW), 0) & 1) == 1
    mid = x_ref[half:half + 1, :FW].astype(F32) * (1.0 / math.sqrt(N))
    o_ref[...] = (y + jnp.where(odd, -mid, mid)).astype(BF16)


def _fourier(fcs, rev, tc, ts, ca, sa):
    nti = N // TMF
    half = N // 2
    return pl.pallas_call(
        _fourier_kernel,
        grid=(B, nti),
        in_specs=[
            pl.BlockSpec((N, 2 * FW), lambda b, i: (b, 0)),
            pl.BlockSpec((REV_BLK, 2 * REV_BLK), lambda b, i: (0, 0)),
            pl.BlockSpec((TMF, half), lambda b, i: (0, 0)),
            pl.BlockSpec((TMF, half), lambda b, i: (0, 0)),
            pl.BlockSpec((None, 1, half), lambda b, i: (i, 0, 0)),
            pl.BlockSpec((None, 1, half), lambda b, i: (i, 0, 0)),
        ],
        out_specs=pl.BlockSpec((TMF, FW), lambda b, i: (b * nti + i, 0)),
        out_shape=jax.ShapeDtypeStruct((LAT, FW), BF16),
        scratch_shapes=[pltpu.VMEM((half, 2 * FW), BF16)],
        compiler_params=_cparams(("arbitrary", "arbitrary")),
        name="fourier",
    )(fcs, rev, tc, ts, ca, sa)


def _fourier_ctx_kernel(x_ref, m_ref, o_ref):
    y = _dot(m_ref[:, :CTX], x_ref[:, :FW]) - _dot(m_ref[:, CTX:], x_ref[:, FW:])
    o_ref[...] = y.astype(BF16)


def _fourier_ctx(fcs, mctx):
    return pl.pallas_call(
        _fourier_ctx_kernel,
        grid=(B,),
        in_specs=[
            pl.BlockSpec((CTX, 2 * FW), lambda b: (LAT // CTX + b, 0)),
            pl.BlockSpec((CTX, 2 * CTX), lambda b: (0, 0)),
        ],
        out_specs=pl.BlockSpec((CTX, FW), lambda b: (b, 0)),
        out_shape=jax.ShapeDtypeStruct((B * CTX, FW), BF16),
        compiler_params=_cparams(("arbitrary",)),
        name="fourier_ctx",
    )(fcs, mctx)


def _na_bias_kernel(rpb_ref, o_ref):
    j = pl.program_id(0)
    p = pl.program_id(1)
    cq = lax.broadcasted_iota(jnp.int32, (GW, 2 * GW), 0)
    lane = lax.broadcasted_iota(jnp.int32, (GW, 2 * GW), 1)
    ck = lane & (GW - 1)
    left = lane < GW
    cs = jnp.clip(cq - NA_KW // 2, 0, GW - NA_KW)
    col_ok = (ck >= cs) & (ck < cs + NA_KW)
    bidx = ck - cq + (NA_KW - 1)
    zero = jnp.zeros((GW, 2 * GW), F32)
    neg = jnp.full((GW, 2 * GW), NEG, F32)
    neg_l = jnp.where(left, neg, zero)
    neg_r = jnp.where(left, zero, neg)
    nb = 2 * NA_KW - 1
    na = 2 * NA_KH - 1
    for hh in range(2):
        base = ((j * 8 + 2 * p + hh) * na) * nb
        halves_l, halves_r = [], []
        for a in range(na):
            r = zero
            for b in range(nb):
                r = jnp.where(bidx == b, rpb_ref[base + a * nb + b], r)
            r = jnp.where(col_ok, r * LOG2E, neg)
            halves_l.append(jnp.where(left, r, zero))
            halves_r.append(jnp.where(left, zero, r))
        for cls, r0 in enumerate((0, QROWS, GH - QROWS)):
            start = min(max(r0 - NA_KH // 2, 0), GH - NA_WIN_ROWS)
            for rql in range(QROWS):
                rq = r0 + rql
                rs = min(max(rq - NA_KH // 2, 0), GH - NA_KH)
                for m in range(NA_WIN_ROWS // 2):
                    rk0 = start + 2 * m
                    rk1 = rk0 + 1
                    tl = halves_l[rk0 - rq + NA_KH - 1] if rs <= rk0 < rs + NA_KH else neg_l
                    tr = halves_r[rk1 - rq + NA_KH - 1] if rs <= rk1 < rs + NA_KH else neg_r
                    o_ref[cls, hh, rql * GW:(rql + 1) * GW, m * 128:(m + 1) * 128] = (tl + tr).astype(BF16)


def _na_bias(rpb_flat):
    n_even = (DEPTH + 1) // 2
    return pl.pallas_call(
        _na_bias_kernel,
        grid=(n_even, 4),
        in_specs=[pl.BlockSpec(memory_space=pltpu.SMEM)],
        out_specs=pl.BlockSpec((None, None, 3, 2, TQ, NA_WIN), lambda j, p: (j, p, 0, 0, 0, 0)),
        out_shape=jax.ShapeDtypeStruct((n_even, 4, 3, 2, TQ, NA_WIN), BF16),
        compiler_params=_cparams(("arbitrary", "arbitrary")),
        name="na_bias",
    )(rpb_flat)


def _na_kernel(q_ref, k_ref, v_ref, kc_ref, vc_ref, bias_ref, o_ref, s_buf, m_buf):
    nq = N // TQ
    nwc = NA_WIN // NA_CK
    lo = _lane_lo()
    kc = kc_ref[...]
    vc = vc_ref[...]

    def win_start(t):
        return pl.multiple_of(jnp.clip(t * QROWS - NA_KH // 2, 0, GH - NA_WIN_ROWS) * GW, 256)

    def stage(qk_t, qk_h, pv_t, pv_h):
        if qk_h is not None:
            q = q_ref[pl.ds(pl.multiple_of(qk_t * TQ, TQ), TQ), :]
            qh = jnp.where(lo if qk_h == 0 else jnp.logical_not(lo), q, jnp.zeros_like(q))
            k0 = win_start(qk_t)
            cls = jnp.where(qk_t == 0, 0, jnp.where(qk_t == nq - 1, 2, 1))
            m128 = None
        if pv_h is not None:
            m = jnp.max(m_buf[pv_h], axis=1, keepdims=True)
            v0 = win_start(pv_t)
            own = lo if pv_h == 0 else jnp.logical_not(lo)
            acc = None
        for c in range(nwc + 1):
            cols = slice(c * NA_CK, (c + 1) * NA_CK)
            if qk_h is not None:
                if c < nwc:
                    s = _dot_t(qh, k_ref[pl.ds(k0 + c * NA_CK, NA_CK), :])
                    s = s + bias_ref[cls, qk_h, :, cols].astype(F32)
                else:
                    s = _dot_t(qh, kc)
                s_buf[qk_h, :, cols] = s
                cm = jnp.maximum(s[:, :128], s[:, 128:])
                m128 = cm if m128 is None else jnp.maximum(m128, cm)
            if pv_h is not None:
                p = jnp.exp2((s_buf[pv_h, :, cols] - m).astype(BF16))
                vb = v_ref[pl.ds(v0 + c * NA_CK, NA_CK), :] if c < nwc else vc
                d = _dot(p, jnp.where(own, vb, jnp.ones_like(vb)))
                acc = d if acc is None else acc + d
        if qk_h is not None:
            m_buf[qk_h] = m128
        if pv_h is not None:
            return acc / pltpu.roll(acc, HD, 1)
        return None

    def finish(t, o0, o1):
        o_ref[pl.ds(pl.multiple_of(t * TQ, TQ), TQ), :] = jnp.where(lo, o0, o1).astype(BF16)

    def tile_body(t, carry):
        o0 = stage(t, 1, t, 0)
        finish(t, o0, stage(jnp.minimum(t + 1, nq - 1), 0, t, 1))
        return carry

    stage(0, 0, None, None)
    lax.fori_loop(0, nq, tile_body, 0)


def _na(qkv, bias, j):
    return pl.pallas_call(
        _na_kernel,
        grid=(4, B),
        in_specs=[
            pl.BlockSpec((N, 128), lambda p, b: (b, p)),
            pl.BlockSpec((N, 128), lambda p, b: (b, 4 + p)),
            pl.BlockSpec((N, 128), lambda p, b: (b, 8 + p)),
            pl.BlockSpec((CTX, 128), lambda p, b: (LAT // CTX + b, 4 + p)),
            pl.BlockSpec((CTX, 128), lambda p, b: (LAT // CTX + b, 8 + p)),
            pl.BlockSpec((None, None, 3, 2, TQ, NA_WIN), lambda p, b: (j, p, 0, 0, 0, 0)),
        ],
        out_specs=pl.BlockSpec((N, 128), lambda p, b: (b, p)),
        out_shape=jax.ShapeDtypeStruct((LAT, NAW), BF16),
        scratch_shapes=[
            pltpu.VMEM((2, TQ, NA_WIN + CTX), F32),
            pltpu.VMEM((2, TQ, 128), F32),
        ],
        compiler_params=_cparams(("arbitrary", "arbitrary")),
        name="na",
    )(qkv, qkv, qkv, qkv, qkv, bias)


def _na_ctx_kernel(q_ref, kc_ref, vc_ref, o_ref):
    lo = _lane_lo()
    for p in range(NAW // 128):
        pair = slice(p * 128, (p + 1) * 128)
        q = q_ref[:, pair]
        kc = kc_ref[:, pair]
        vc = vc_ref[:, pair]
        outs = []
        for hh in range(2):
            own = lo if hh == 0 else jnp.logical_not(lo)
            qh = jnp.where(own, q, jnp.zeros_like(q))
            o = _softmax_pv(_dot_t(qh, kc), jnp.where(own, vc, jnp.ones_like(vc)))
            outs.append(o / pltpu.roll(o, HD, 1))
        o_ref[:, pair] = jnp.where(lo, outs[0], outs[1]).astype(BF16)


def _na_ctx(qkv):
    r0 = LAT // CTX
    return pl.pallas_call(
        _na_ctx_kernel,
        grid=(B,),
        in_specs=[
            pl.BlockSpec((CTX, NAW), lambda b: (r0 + b, 0)),
            pl.BlockSpec((CTX, NAW), lambda b: (r0 + b, 1)),
            pl.BlockSpec((CTX, NAW), lambda b: (r0 + b, 2)),
        ],
        out_specs=pl.BlockSpec((CTX, NAW), lambda b: (b, 0)),
        out_shape=jax.ShapeDtypeStruct((B * CTX, NAW), BF16),
        compiler_params=_cparams(("arbitrary",)),
        name="na_ctx",
    )(qkv, qkv, qkv)


def _diff_lambda(lq1_ref, lk1_ref, lq2_ref, lk2_ref, lam_init):
    s1 = jnp.sum(lq1_ref[...] * lk1_ref[...], axis=1, keepdims=True)
    s2 = jnp.sum(lq2_ref[...] * lk2_ref[...], axis=1, keepdims=True)
    return jnp.exp(s1) - jnp.exp(s2) + lam_init


def _diff_finish(o1, o2, lam, g, lam_init):
    o = o1 - lam * o2
    ms = jnp.mean(o * o, axis=-1, keepdims=True)
    return (o * lax.rsqrt(ms + SUBLN_EPS) * g * (1.0 - lam_init)).astype(BF16)


def _diff_kernel(lam_init, q_ref, kl_ref, vl_ref, kc_ref, vc_ref, lq1_ref, lk1_ref, lq2_ref, lk2_ref,
                 g_ref, o_ref, k_all, vt_ext, s_buf, m_buf):
    k_all[0:CTX, :] = kc_ref[...]
    k_all[CTX:, :] = kl_ref[...]
    vt_ext[0:128, 0:CTX] = vc_ref[...].astype(F32).T.astype(BF16)
    vt_ext[0:128, CTX:] = vl_ref[...].astype(F32).T.astype(BF16)
    vt_ext[128:, :] = jnp.ones((DIFF_SUM_ROWS, CTX + N), BF16)

    lam = _diff_lambda(lq1_ref, lk1_ref, lq2_ref, lk2_ref, lam_init)
    g_out = g_ref[...] * (1.0 - lam_init)
    lo = _lane_lo()
    nq = N // TQ

    def stage(qk_tile, qk_slot, pv_slot):
        if qk_slot is not None:
            q = q_ref[pl.ds(pl.multiple_of(qk_tile * TQ, TQ), TQ), :]
            qj = jnp.where(lo if qk_slot == 0 else jnp.logical_not(lo), q, jnp.zeros_like(q))
            m8 = None
        if pv_slot is not None:
            m = jnp.max(m_buf[pv_slot], axis=0, keepdims=True)
            acc = None

        def numerators(c, m_c):
            rows = slice(DIFF_CHUNKS[c], DIFF_CHUNKS[c + 1])
            return jnp.exp2((s_buf[pv_slot, rows, :] - m_c).astype(BF16))

        nck = len(DIFF_CHUNKS) - 1
        p_next = numerators(0, m) if pv_slot is not None else None
        cm_prev = None
        for c in range(nck):
            rows = slice(DIFF_CHUNKS[c], DIFF_CHUNKS[c + 1])
            m_c = m if pv_slot is not None else None
            if qk_slot is not None:
                if pv_slot is not None and cm_prev is not None:
                    m_c = jnp.maximum(m, jnp.minimum(cm_prev[0:1, :], m))
                s = _dot_t(k_all[rows, :], qj)
                s_buf[qk_slot, rows, :] = s
                cm_prev = jnp.max(s.reshape(-1, 8, TQ), axis=0)
                m8 = cm_prev if m8 is None else jnp.maximum(m8, cm_prev)
            if pv_slot is not None:
                p = p_next
                if c + 1 < nck:
                    p_next = numerators(c + 1, m_c)
                d = _dot(vt_ext[:, rows], p)
                acc = d if acc is None else acc + d
        if qk_slot is not None:
            m_buf[qk_slot] = m8
        if pv_slot is not None:
            return acc[0:128, :] / acc[128:129, :]
        return None

    def finish(t, o1, o2):
        o = o1 - lam * o2
        ms = jnp.mean(o * o, axis=0, keepdims=True)
        on = (o * lax.rsqrt(ms + SUBLN_EPS)).T
        o_ref[pl.ds(pl.multiple_of(t * TQ, TQ), TQ), :] = (on * g_out).astype(BF16)

    def tile_body(t, carry):
        o1 = stage(t, 1, 0)
        finish(t, o1, stage(jnp.minimum(t + 1, nq - 1), 0, 1))
        return carry

    stage(0, 0, None)
    lax.fori_loop(0, nq, tile_body, 0)


def _diff_ctx_kernel(lam_init, q_ref, kc_ref, vc_ref, lq1_ref, lk1_ref, lq2_ref, lk2_ref, g_ref, o_ref):
    lam = _diff_lambda(lq1_ref, lk1_ref, lq2_ref, lk2_ref, lam_init)
    lo = _lane_lo()
    for h in range(D // 128):
        head = slice(h * 128, (h + 1) * 128)
        q = q_ref[:, head]
        kc = kc_ref[:, head]
        vc = vc_ref[:, head]
        v_ext = jnp.concatenate([vc, jnp.ones_like(vc)], axis=1)
        outs = []
        for jj in range(2):
            qj = jnp.where(lo if jj == 0 else jnp.logical_not(lo), q, jnp.zeros_like(q))
            o = _softmax_pv(_dot_t(qj, kc), v_ext)
            outs.append(o[:, :128] / o[:, 128:])
        o_ref[:, head] = _diff_finish(outs[0], outs[1], lam, g_ref[...], lam_init)


def _diff(qkv, lams, g, j, lam_init):
    nh = D // 128
    r0 = LAT // CTX
    return pl.pallas_call(
        functools.partial(_diff_kernel, lam_init),
        grid=(B, nh),
        in_specs=[
            pl.BlockSpec((N, 128), lambda b, h: (b, h)),
            pl.BlockSpec((N, 128), lambda b, h: (b, nh + h)),
            pl.BlockSpec((N, 128), lambda b, h: (b, 2 * nh + h)),
            pl.BlockSpec((CTX, 128), lambda b, h: (r0 + b, nh + h)),
            pl.BlockSpec((CTX, 128), lambda b, h: (r0 + b, 2 * nh + h)),
        ] + [_layer_spec((1, HD), j)] * 4 + [_layer_spec((1, 2 * HD), j)],
        out_specs=pl.BlockSpec((N, 128), lambda b, h: (b, h)),
        out_shape=jax.ShapeDtypeStruct((LAT, D), BF16),
        scratch_shapes=[
            pltpu.VMEM((CTX + N, 128), BF16),
            pltpu.VMEM((128 + DIFF_SUM_ROWS, CTX + N), BF16),
            pltpu.VMEM((2, CTX + N, TQ), F32),
            pltpu.VMEM((2, 8, TQ), F32),
        ],
        compiler_params=_cparams(("arbitrary", "arbitrary")),
        name="diff_attn",
    )(qkv, qkv, qkv, qkv, qkv, *lams, g)


def _diff_ctx(qkv, lams, g, j, lam_init):
    r0 = LAT // CTX
    return pl.pallas_call(
        functools.partial(_diff_ctx_kernel, lam_init),
        grid=(B,),
        in_specs=[
            pl.BlockSpec((CTX, D), lambda b: (r0 + b, 0)),
            pl.BlockSpec((CTX, D), lambda b: (r0 + b, 1)),
            pl.BlockSpec((CTX, D), lambda b: (r0 + b, 2)),
        ] + [_layer_spec((1, HD), j)] * 4 + [_layer_spec((1, 2 * HD), j)],
        out_specs=pl.BlockSpec((CTX, D), lambda b: (b, 0)),
        out_shape=jax.ShapeDtypeStruct((B * CTX, D), BF16),
        compiler_params=_cparams(("arbitrary",)),
        name="diff_attn_ctx",
    )(qkv, qkv, qkv, *lams, g)


def _post_kernel(last, n_x, widths, *refs):
    n_att = len(widths) * (1 if last else 2)
    x_refs = refs[:n_x]
    att_refs = refs[n_x:n_x + n_att]
    mod_ref, g_ref, wo_ref, wi_ref, w2_ref, fg_ref, o_ref = refs[n_x + n_att:]
    x = _load_rows(x_refs)
    is_latent = pl.program_id(0) < NT_LAT
    y = None
    off = 0
    for i, w in enumerate(widths):
        if last:
            a = att_refs[i][...]
        else:
            a = jnp.where(is_latent, att_refs[2 * i][...], att_refs[2 * i + 1][...])
        d = _dot(a, wo_ref[off:off + w, :])
        y = d if y is None else y + d
        off += w
    x2 = x + mod_ref[2:3, :] * y
    h = _norm_mod(x2, g_ref[...], mod_ref[3:4, :], mod_ref[4:5, :]).astype(BF16)
    acc = jnp.zeros((TM, D), F32)
    for c in range(MLP_H // D):
        u = jnp.maximum(_dot(h, wi_ref[:, c * D:(c + 1) * D]), 0.0)
        acc = acc + _dot((u * u).astype(BF16), w2_ref[c * D:(c + 1) * D, :])
    x3 = x2 + mod_ref[5:6, :] * acc
    if last:
        ms = jnp.mean(x3 * x3, axis=-1, keepdims=True)
        x3 = x3 * lax.rsqrt(ms + NORM_EPS) * fg_ref[...]
    o_ref[...] = x3


def _post(xs, att, mods, i, g_all, wo_all, j, wi_all, w2_all, fg, last):
    nt = NT_LAT if last else NT
    x_specs, x_args = _x_specs(xs)
    widths = tuple(a_lat.shape[1] for a_lat, _ in att)
    assert sum(widths) == D
    att_specs, att_args = [], []
    for (a_lat, a_ctx), w in zip(att, widths):
        att_specs.append(pl.BlockSpec((TM, w), lambda t: (jnp.minimum(t, NT_LAT - 1), 0)))
        att_args.append(a_lat)
        if not last:
            att_specs.append(pl.BlockSpec((TM, w), lambda t: (jnp.maximum(t - NT_LAT, 0), 0)))
            att_args.append(a_ctx)
    once = dict(pipeline_mode=pl.Buffered(1))
    return pl.pallas_call(
        functools.partial(_post_kernel, last, len(x_args), widths),
        grid=(nt,),
        in_specs=x_specs + att_specs + [
            _mod_spec(i),
            _layer_spec((1, D), i),
            _layer_spec((D, D), j, **once),
            _layer_spec((D, MLP_H), i, **once),
            _layer_spec((MLP_H, D), i, **once),
            pl.BlockSpec((1, D), lambda t: (0, 0)),
        ],
        out_specs=pl.BlockSpec((TM, D), lambda t: (t, 0)),
        out_shape=jax.ShapeDtypeStruct((nt * TM, D), F32),
        compiler_params=_cparams(("arbitrary",)),
        name="post_last" if last else "post",
    )(*x_args, *att_args, mods, g_all, wo_all, wi_all, w2_all, fg)


def _channel_dft():
    n = np.arange(FD)
    ang = 2.0 * np.pi * ((n[:, None] * n[None, :]) % FD) / FD
    c = np.cos(ang) / math.sqrt(FD)
    s = np.sin(ang) / math.sqrt(FD)
    m = np.zeros((FW, 2 * FW), np.float32)
    for g in range(FG):
        m[g * FD:(g + 1) * FD, g * FD:(g + 1) * FD] = c
        m[g * FD:(g + 1) * FD, FW + g * FD:FW + (g + 1) * FD] = s
    return jnp.asarray(m).astype(BF16)


def _ctx_dft():
    n = np.arange(CTX)
    ang = 2.0 * np.pi * ((n[:, None] * n[None, :]) % CTX) / CTX
    m = np.concatenate([np.cos(ang), np.sin(ang)], axis=1) / math.sqrt(CTX)
    return jnp.asarray(m.astype(np.float32)).astype(BF16)


def _row_reversal():
    m = np.zeros((REV_BLK, 2 * REV_BLK), np.float32)
    m[np.arange(REV_BLK), REV_BLK - np.arange(REV_BLK)] = 1.0
    return jnp.asarray(m).astype(BF16)


def _position_dft_tables():
    half = N // 2
    l = jnp.arange(half, dtype=jnp.int32)[None, :]
    dk = jnp.arange(TMF, dtype=jnp.int32)[:, None]
    ang = ((dk * l) % N).astype(F32) * (2.0 * math.pi / N)
    k0 = (jnp.arange(N // TMF, dtype=jnp.int32) * TMF)[:, None]
    ang0 = ((k0 * l) % N).astype(F32) * (2.0 * math.pi / N)
    scale = 1.0 / math.sqrt(N)
    ca = (jnp.cos(ang0) * scale).reshape(N // TMF, 1, half)
    sa = (jnp.sin(ang0) * scale).reshape(N // TMF, 1, half)
    return jnp.cos(ang), jnp.sin(ang), ca, sa


def _rope_tables():
    t = jnp.arange(N)
    rows, cols = t // GW, t % GW
    nf = HD // 4
    inv_freq = ROPE_THETA ** (-jnp.arange(nf, dtype=F32) / nf)
    lane = np.arange(128)
    d = lane % HD
    use_col = d >= HD // 2
    first = (d % (HD // 2)) < nf
    pos = jnp.where(use_col[None, :], cols[:, None], rows[:, None]).astype(F32)
    ang = pos * inv_freq[d % nf][None, :]
    cos, sin = jnp.cos(ang), jnp.sin(ang)
    sa = jnp.where(first[None, :], -sin, 0.0)
    sb = jnp.where(first[None, :], 0.0, sin)
    pad = jnp.zeros((TM, 128), F32)
    return (jnp.concatenate([cos, pad + 1.0]), jnp.concatenate([sa, pad]), jnp.concatenate([sb, pad]))


def kernel(x, c, ctx, c_ctx, w_mod, b_mod, norm_mix_g, norm_mlp_g, w_mlp_in, w_mlp_out, w_in_ab, w_out_ab,
           na_rpb, w_qkv_diff, w_out_diff, diff_lq1, diff_lk1, diff_lq2, diff_lk2, diff_subln_g,
           final_norm_g):
    assert x.shape == (B, N, D) and ctx.shape == (B, CTX, D)
    xs = (x.reshape(LAT, D), ctx.reshape(B * CTX, D))
    cin = jnp.concatenate([c, c_ctx[None, :], jnp.zeros((8 - B - 1, D), F32)], axis=0)
    mods = _modulation(cin, w_mod, b_mod).reshape(DEPTH, 8, 6, D)

    cs = _channel_dft()
    mctx = _ctx_dft()
    rev = _row_reversal()
    tc, ts, ca, sa = _position_dft_tables()
    rope = _rope_tables()
    bias = _na_bias(na_rpb.reshape(-1))
    fg = final_norm_g.reshape(1, D)

    g_mix = norm_mix_g.reshape(DEPTH, 1, D)
    g_mlp = norm_mlp_g.reshape(DEPTH, 1, D)
    w_ab, wo_ab = w_in_ab.astype(BF16), w_out_ab.astype(BF16)
    w_qkv, wo_diff = w_qkv_diff.astype(BF16), w_out_diff.astype(BF16)
    w_in, w_out = w_mlp_in.astype(BF16), w_mlp_out.astype(BF16)
    lams = [v.reshape(-1, 1, HD) for v in (diff_lq1, diff_lk1, diff_lq2, diff_lk2)]
    g_sub = diff_subln_g.reshape(-1, 1, 2 * HD)

    for i in range(DEPTH):
        last = i == DEPTH - 1
        j = i // 2
        if i % 2 == 0:
            fcs, qkv = _proj_even(xs, mods, i, g_mix, w_ab, j, cs)
            att = [(_fourier(fcs, rev, tc, ts, ca, sa), None if last else _fourier_ctx(fcs, mctx)),
                   (_na(qkv, bias, j), None if last else _na_ctx(qkv))]
            wo = wo_ab
        else:
            lam_init = 0.8 - 0.6 * math.exp(-0.3 * i)
            qkv = _proj_odd(xs, mods, i, g_mix, w_qkv, j, *rope)
            att = [(_diff(qkv, lams, g_sub, j, lam_init),
                    None if last else _diff_ctx(qkv, lams, g_sub, j, lam_init))]
            wo = wo_diff
        xs = _post(xs, att, mods, i, g_mlp, wo, j, w_in, w_out, fg, last)
    return xs.reshape(B, N, D)
```

```python
import functools
import math

import numpy as np
import jax
import jax.numpy as jnp
from jax import lax
from jax.experimental import pallas as pl
from jax.experimental.pallas import tpu as pltpu

F32 = jnp.float32
BF16 = jnp.bfloat16

D = 1024
B = 4
N = 4096
CTX = 256
DEPTH = 4
GW = 64
GH = N // GW
FW = 512
FG = 4
FD = 128
NAW = 512
HD = 64
NA_KH = 8
NA_KW = 16
MLP_H = 4 * D
NORM_EPS = 1e-6
SUBLN_EPS = 1e-5
NEG = -1e30
ROPE_THETA = 10000.0
LOG2E = 1.4426950408889634
Q_SCALE = HD ** -0.5 * LOG2E

TM = 512
LAT = B * N
ROWS = LAT + B * CTX
NT = ROWS // TM
NT_LAT = LAT // TM
TILES_PER_BATCH = N // TM
TQ = 512
QROWS = TQ // GW
NA_WIN_ROWS = 16
NA_WIN = NA_WIN_ROWS * GW
NA_CK = 256
NA_UNROLL = 4
DIFF_UNROLL = 2
TMF = 512
REV_BLK = 256
DIFF_SUM_ROWS = 16
DIFF_CHUNKS = tuple(range(0, CTX + N + 1, 256))

VMEM_LIMIT = 60 * 1024 * 1024


def _cparams(sem):
    return pltpu.CompilerParams(dimension_semantics=sem, vmem_limit_bytes=VMEM_LIMIT)


def _mod_kernel(cin_ref, w_ref, b_ref, o_ref):
    cin = cin_ref[...]
    act = cin * jax.nn.sigmoid(cin)
    a_hi = act.astype(BF16).astype(F32)
    a2 = jnp.concatenate([a_hi, act - a_hi], axis=0).astype(BF16)
    w = w_ref[...]
    w_hi = w.astype(BF16)
    w_lo = (w - w_hi.astype(F32)).astype(BF16)
    r = _dot(a2, w_hi)
    o_ref[...] = r[0:8, :] + r[8:16, :] + _dot(a_hi.astype(BF16), w_lo) + b_ref[...]


def _modulation(cin, w_mod, b_mod):
    tn = 3072
    return pl.pallas_call(
        _mod_kernel,
        grid=(DEPTH, 6 * D // tn),
        in_specs=[
            pl.BlockSpec((8, D), lambda l, j: (0, 0)),
            pl.BlockSpec((None, D, tn), lambda l, j: (l, 0, j)),
            pl.BlockSpec((None, 1, tn), lambda l, j: (l, 0, j)),
        ],
        out_specs=pl.BlockSpec((None, 8, tn), lambda l, j: (l, 0, j)),
        out_shape=jax.ShapeDtypeStruct((DEPTH, 8, 6 * D), F32),
        compiler_params=_cparams(("arbitrary", "arbitrary")),
        name="modulation",
    )(cin, w_mod, b_mod.reshape(DEPTH, 1, 6 * D))


def _norm_mod(x, g, shift, scale):
    ms = jnp.mean(x * x, axis=-1, keepdims=True)
    y = x * lax.rsqrt(ms + NORM_EPS) * g
    return y * (1.0 + scale) + shift


def _dot(a, b):
    return jnp.dot(a, b, preferred_element_type=F32)


def _dot_t(a, b):
    return lax.dot_general(a, b, (((1,), (1,)), ((), ())), preferred_element_type=F32)


def _softmax_pv(s, v_ext):
    m = jnp.max(s, axis=1, keepdims=True)
    p = jnp.exp2((s - m).astype(BF16))
    return _dot(p, v_ext)


def _lane_lo():
    return lax.broadcasted_iota(jnp.int32, (1, 2 * HD), 1) < HD


def _x_specs(xs):
    if isinstance(xs, tuple):
        return ([pl.BlockSpec((TM, D), lambda t: (jnp.minimum(t, NT_LAT - 1), 0)),
                 pl.BlockSpec((TM, D), lambda t: (jnp.maximum(t - NT_LAT, 0), 0))], list(xs))
    return [pl.BlockSpec((TM, D), lambda t: (t, 0))], [xs]


def _load_rows(refs):
    if len(refs) == 1:
        return refs[0][...]
    return jnp.where(pl.program_id(0) < NT_LAT, refs[0][...], refs[1][...])


def _layer_spec(shape, layer, **kw):
    return pl.BlockSpec((None,) + shape, lambda *_: (layer,) + (0,) * len(shape), **kw)


def _mod_spec(layer):
    return pl.BlockSpec((None, None, 6, D), lambda t: (layer, t // TILES_PER_BATCH, 0, 0))


def _proj_even_kernel(n_x, *refs):
    x_refs = refs[:n_x]
    mod_ref, g_ref, w_ref, cs_ref, fcs_ref, qkv_ref = refs[n_x:]
    h = _norm_mod(_load_rows(x_refs), g_ref[...], mod_ref[0:1, :], mod_ref[1:2, :]).astype(BF16)
    p = _dot(h, w_ref[...])
    f = p[:, :FW].astype(BF16)
    for g in range(FG):
        r = _dot(f[:, g * FD:(g + 1) * FD], cs_ref[...])
        fcs_ref[:, g * FD:(g + 1) * FD] = r[:, :FD].astype(BF16)
        fcs_ref[:, FW + g * FD:FW + (g + 1) * FD] = r[:, FD:].astype(BF16)
    qkv_ref[:, :NAW] = (p[:, FW:FW + NAW] * Q_SCALE).astype(BF16)
    qkv_ref[:, NAW:] = p[:, FW + NAW:].astype(BF16)


def _proj_even(xs, mods, i, g_all, w_all, j, cs):
    nout = FW + 3 * NAW
    x_specs, x_args = _x_specs(xs)
    return pl.pallas_call(
        functools.partial(_proj_even_kernel, len(x_args)),
        grid=(NT,),
        in_specs=x_specs + [
            _mod_spec(i),
            _layer_spec((1, D), i),
            _layer_spec((D, nout), j),
            pl.BlockSpec((FD, 2 * FD), lambda t: (0, 0)),
        ],
        out_specs=[
            pl.BlockSpec((TM, 2 * FW), lambda t: (t, 0)),
            pl.BlockSpec((TM, 3 * NAW), lambda t: (t, 0)),
        ],
        out_shape=[
            jax.ShapeDtypeStruct((ROWS, 2 * FW), BF16),
            jax.ShapeDtypeStruct((ROWS, 3 * NAW), BF16),
        ],
        compiler_params=_cparams(("arbitrary",)),
        name="proj_even",
    )(*x_args, mods, g_all, w_all, cs)


def _proj_odd_kernel(x_ref, mod_ref, g_ref, w_ref, cos_ref, sa_ref, sb_ref, qkv_ref):
    h = _norm_mod(x_ref[...], g_ref[...], mod_ref[0:1, :], mod_ref[1:2, :]).astype(BF16)
    p = _dot(h, w_ref[...])
    cos = cos_ref[...]
    sa = sa_ref[...]
    sb = sb_ref[...]
    for c in range(2 * D // 128):
        xc = p[:, c * 128:(c + 1) * 128]
        r = xc * cos + pltpu.roll(xc, 112, 1) * sa + pltpu.roll(xc, 16, 1) * sb
        if c < D // 128:
            r = r * Q_SCALE
        qkv_ref[:, c * 128:(c + 1) * 128] = r.astype(BF16)
    qkv_ref[:, 2 * D:] = p[:, 2 * D:].astype(BF16)


def _proj_odd(xs, mods, i, g_all, w_all, j, cos, sa, sb):
    nout = 3 * D

    def rope_idx(t):
        return (jnp.where(t < NT_LAT, t % TILES_PER_BATCH, TILES_PER_BATCH), 0)

    return pl.pallas_call(
        _proj_odd_kernel,
        grid=(NT,),
        in_specs=[
            pl.BlockSpec((TM, D), lambda t: (t, 0)),
            _mod_spec(i),
            _layer_spec((1, D), i),
            _layer_spec((D, nout), j),
            pl.BlockSpec((TM, 128), rope_idx),
            pl.BlockSpec((TM, 128), rope_idx),
            pl.BlockSpec((TM, 128), rope_idx),
        ],
        out_specs=pl.BlockSpec((TM, nout), lambda t: (t, 0)),
        out_shape=jax.ShapeDtypeStruct((ROWS, nout), BF16),
        compiler_params=_cparams(("arbitrary",)),
        name="proj_odd",
    )(xs, mods, g_all, w_all, cos, sa, sb)


def _fourier_kernel(x_ref, rev_ref, tc_ref, ts_ref, ca_ref, sa_ref, o_ref, eo):
    half = N // 2
    rb = REV_BLK

    @pl.when(pl.program_id(1) == 0)
    def _():
        first = lax.broadcasted_iota(jnp.int32, (rb, FW), 0) == 0
        for l0 in range(0, half, rb):
            if l0 == 0:
                win = jnp.concatenate([x_ref[N - rb:N, :], x_ref[0:rb, :]], axis=0)
            else:
                win = x_ref[N - l0 - rb:N - l0 + rb, :]
            xr = _dot(rev_ref[...], win)
            xf = x_ref[l0:l0 + rb, :].astype(F32)
            e = xf[:, :FW] + xr[:, :FW]
            if l0 == 0:
                e = jnp.where(first, xf[:, :FW], e)
            eo[l0:l0 + rb, :FW] = e.astype(BF16)
            eo[l0:l0 + rb, FW:] = (xf[:, FW:] - xr[:, FW:]).astype(BF16)

    tc = tc_ref[...]
    ts = ts_ref[...]
    ca = ca_ref[...]
    sa = sa_ref[...]
    mc = (tc * ca - ts * sa).astype(BF16)
    ms = (ts * ca + tc * sa).astype(BF16)
    y = _dot(mc, eo[:, :FW]) - _dot(ms, eo[:, FW:])
    odd = (lax.broadcasted_iota(jnp.int32, (TMF, FW), 0) & 1) == 1
    mid = x_ref[half:half + 1, :FW].astype(F32) * (1.0 / math.sqrt(N))
    o_ref[...] = (y + jnp.where(odd, -mid, mid)).astype(BF16)


def _fourier(fcs, rev, tc, ts, ca, sa):
    nti = N // TMF
    half = N // 2
    return pl.pallas_call(
        _fourier_kernel,
        grid=(B, nti),
        in_specs=[
            pl.BlockSpec((N, 2 * FW), lambda b, i: (b, 0)),
            pl.BlockSpec((REV_BLK, 2 * REV_BLK), lambda b, i: (0, 0)),
            pl.BlockSpec((TMF, half), lambda b, i: (0, 0)),
            pl.BlockSpec((TMF, half), lambda b, i: (0, 0)),
            pl.BlockSpec((None, 1, half), lambda b, i: (i, 0, 0)),
            pl.BlockSpec((None, 1, half), lambda b, i: (i, 0, 0)),
        ],
        out_specs=pl.BlockSpec((TMF, FW), lambda b, i: (b * nti + i, 0)),
        out_shape=jax.ShapeDtypeStruct((LAT, FW), BF16),
        scratch_shapes=[pltpu.VMEM((half, 2 * FW), BF16)],
        compiler_params=_cparams(("arbitrary", "arbitrary")),
        name="fourier",
    )(fcs, rev, tc, ts, ca, sa)


def _fourier_ctx_kernel(x_ref, m_ref, o_ref):
    y = _dot(m_ref[:, :CTX], x_ref[:, :FW]) - _dot(m_ref[:, CTX:], x_ref[:, FW:])
    o_ref[...] = y.astype(BF16)


def _fourier_ctx(fcs, mctx):
    return pl.pallas_call(
        _fourier_ctx_kernel,
        grid=(B,),
        in_specs=[
            pl.BlockSpec((CTX, 2 * FW), lambda b: (LAT // CTX + b, 0)),
            pl.BlockSpec((CTX, 2 * CTX), lambda b: (0, 0)),
        ],
        out_specs=pl.BlockSpec((CTX, FW), lambda b: (b, 0)),
        out_shape=jax.ShapeDtypeStruct((B * CTX, FW), BF16),
        compiler_params=_cparams(("arbitrary",)),
        name="fourier_ctx",
    )(fcs, mctx)


def _na_bias_kernel(rpb_ref, o_ref):
    j = pl.program_id(0)
    p = pl.program_id(1)
    cq = lax.broadcasted_iota(jnp.int32, (GW, 2 * GW), 0)
    lane = lax.broadcasted_iota(jnp.int32, (GW, 2 * GW), 1)
    ck = lane & (GW - 1)
    left = lane < GW
    cs = jnp.clip(cq - NA_KW // 2, 0, GW - NA_KW)
    col_ok = (ck >= cs) & (ck < cs + NA_KW)
    bidx = ck - cq + (NA_KW - 1)
    zero = jnp.zeros((GW, 2 * GW), F32)
    neg = jnp.full((GW, 2 * GW), NEG, F32)
    neg_l = jnp.where(left, neg, zero)
    neg_r = jnp.where(left, zero, neg)
    nb = 2 * NA_KW - 1
    na = 2 * NA_KH - 1
    for hh in range(2):
        base = ((j * 8 + 2 * p + hh) * na) * nb
        halves_l, halves_r = [], []
        for a in range(na):
            r = zero
            for b in range(nb):
                r = jnp.where(bidx == b, rpb_ref[base + a * nb + b], r)
            r = jnp.where(col_ok, r * LOG2E, neg)
            halves_l.append(jnp.where(left, r, zero))
            halves_r.append(jnp.where(left, zero, r))
        for cls, r0 in enumerate((0, QROWS, GH - QROWS)):
            start = min(max(r0 - NA_KH // 2, 0), GH - NA_WIN_ROWS)
            for rql in range(QROWS):
                rq = r0 + rql
                rs = min(max(rq - NA_KH // 2, 0), GH - NA_KH)
                for m in range(NA_WIN_ROWS // 2):
                    rk0 = start + 2 * m
                    rk1 = rk0 + 1
                    tl = halves_l[rk0 - rq + NA_KH - 1] if rs <= rk0 < rs + NA_KH else neg_l
                    tr = halves_r[rk1 - rq + NA_KH - 1] if rs <= rk1 < rs + NA_KH else neg_r
                    o_ref[cls, hh, rql * GW:(rql + 1) * GW, m * 128:(m + 1) * 128] = (tl + tr).astype(BF16)


def _na_bias(rpb_flat):
    n_even = (DEPTH + 1) // 2
    return pl.pallas_call(
        _na_bias_kernel,
        grid=(n_even, 4),
        in_specs=[pl.BlockSpec(memory_space=pltpu.SMEM)],
        out_specs=pl.BlockSpec((None, None, 3, 2, TQ, NA_WIN), lambda j, p: (j, p, 0, 0, 0, 0)),
        out_shape=jax.ShapeDtypeStruct((n_even, 4, 3, 2, TQ, NA_WIN), BF16),
        compiler_params=_cparams(("arbitrary", "arbitrary")),
        name="na_bias",
    )(rpb_flat)


def _na_kernel(q_ref, k_ref, v_ref, kc_ref, vc_ref, bias_ref, o_ref, s_buf, m_buf):
    nq = N // TQ
    nwc = NA_WIN // NA_CK
    lo = _lane_lo()
    kc = kc_ref[...]
    vc = vc_ref[...]

    def win_start(t):
        return pl.multiple_of(jnp.clip(t * QROWS - NA_KH // 2, 0, GH - NA_WIN_ROWS) * GW, 256)

    def stage(qk_t, qk_h, pv_t, pv_h):
        if qk_h is not None:
            q = q_ref[pl.ds(pl.multiple_of(qk_t * TQ, TQ), TQ), :]
            qh = jnp.where(lo if qk_h == 0 else jnp.logical_not(lo), q, jnp.zeros_like(q))
            k0 = win_start(qk_t)
            cls = jnp.where(qk_t == 0, 0, jnp.where(qk_t == nq - 1, 2, 1))
            m128 = None
        if pv_h is not None:
            m = jnp.max(m_buf[pv_h], axis=1, keepdims=True)
            v0 = win_start(pv_t)
            own = lo if pv_h == 0 else jnp.logical_not(lo)
            acc = None
        for c in range(nwc + 1):
            cols = slice(c * NA_CK, (c + 1) * NA_CK)
            if qk_h is not None:
                if c < nwc:
                    s = _dot_t(qh, k_ref[pl.ds(k0 + c * NA_CK, NA_CK), :])
                    s = s + bias_ref[cls, qk_h, :, cols].astype(F32)
                else:
                    s = _dot_t(qh, kc)
                s_buf[qk_h, :, cols] = s
                cm = jnp.maximum(s[:, :128], s[:, 128:])
                m128 = cm if m128 is None else jnp.maximum(m128, cm)
            if pv_h is not None:
                p = jnp.exp2((s_buf[pv_h, :, cols] - m).astype(BF16))
                vb = v_ref[pl.ds(v0 + c * NA_CK, NA_CK), :] if c < nwc else vc
                d = _dot(p, jnp.where(own, vb, jnp.ones_like(vb)))
                acc = d if acc is None else acc + d
        if qk_h is not None:
            m_buf[qk_h] = m128
        if pv_h is not None:
            return acc / pltpu.roll(acc, HD, 1)
        return None

    def finish(t, o0, o1):
        o_ref[pl.ds(pl.multiple_of(t * TQ, TQ), TQ), :] = jnp.where(lo, o0, o1).astype(BF16)

    def tile_body(t, carry):
        o0 = stage(t, 1, t, 0)
        finish(t, o0, stage(jnp.minimum(t + 1, nq - 1), 0, t, 1))
        return carry

    stage(0, 0, None, None)
    lax.fori_loop(0, nq, tile_body, 0, unroll=NA_UNROLL)


def _na(qkv, bias, j):
    return pl.pallas_call(
        _na_kernel,
        grid=(4, B),
        in_specs=[
            pl.BlockSpec((N, 128), lambda p, b: (b, p)),
            pl.BlockSpec((N, 128), lambda p, b: (b, 4 + p)),
            pl.BlockSpec((N, 128), lambda p, b: (b, 8 + p)),
            pl.BlockSpec((CTX, 128), lambda p, b: (LAT // CTX + b, 4 + p)),
            pl.BlockSpec((CTX, 128), lambda p, b: (LAT // CTX + b, 8 + p)),
            pl.BlockSpec((None, None, 3, 2, TQ, NA_WIN), lambda p, b: (j, p, 0, 0, 0, 0)),
        ],
        out_specs=pl.BlockSpec((N, 128), lambda p, b: (b, p)),
        out_shape=jax.ShapeDtypeStruct((LAT, NAW), BF16),
        scratch_shapes=[
            pltpu.VMEM((2, TQ, NA_WIN + CTX), F32),
            pltpu.VMEM((2, TQ, 128), F32),
        ],
        compiler_params=_cparams(("arbitrary", "arbitrary")),
        name="na",
    )(qkv, qkv, qkv, qkv, qkv, bias)


def _na_ctx_kernel(q_ref, kc_ref, vc_ref, o_ref):
    lo = _lane_lo()
    for p in range(NAW // 128):
        pair = slice(p * 128, (p + 1) * 128)
        q = q_ref[:, pair]
        kc = kc_ref[:, pair]
        vc = vc_ref[:, pair]
        outs = []
        for hh in range(2):
            own = lo if hh == 0 else jnp.logical_not(lo)
            qh = jnp.where(own, q, jnp.zeros_like(q))
            o = _softmax_pv(_dot_t(qh, kc), jnp.where(own, vc, jnp.ones_like(vc)))
            outs.append(o / pltpu.roll(o, HD, 1))
        o_ref[:, pair] = jnp.where(lo, outs[0], outs[1]).astype(BF16)


def _na_ctx(qkv):
    r0 = LAT // CTX
    return pl.pallas_call(
        _na_ctx_kernel,
        grid=(B,),
        in_specs=[
            pl.BlockSpec((CTX, NAW), lambda b: (r0 + b, 0)),
            pl.BlockSpec((CTX, NAW), lambda b: (r0 + b, 1)),
            pl.BlockSpec((CTX, NAW), lambda b: (r0 + b, 2)),
        ],
        out_specs=pl.BlockSpec((CTX, NAW), lambda b: (b, 0)),
        out_shape=jax.ShapeDtypeStruct((B * CTX, NAW), BF16),
        compiler_params=_cparams(("arbitrary",)),
        name="na_ctx",
    )(qkv, qkv, qkv)


def _diff_lambda(lq1_ref, lk1_ref, lq2_ref, lk2_ref, lam_init):
    s1 = jnp.sum(lq1_ref[...] * lk1_ref[...], axis=1, keepdims=True)
    s2 = jnp.sum(lq2_ref[...] * lk2_ref[...], axis=1, keepdims=True)
    return jnp.exp(s1) - jnp.exp(s2) + lam_init


def _diff_finish(o1, o2, lam, g, lam_init):
    o = o1 - lam * o2
    ms = jnp.mean(o * o, axis=-1, keepdims=True)
    return (o * lax.rsqrt(ms + SUBLN_EPS) * g * (1.0 - lam_init)).astype(BF16)


def _diff_kernel(lam_init, q_ref, kl_ref, vl_ref, kc_ref, vc_ref, lq1_ref, lk1_ref, lq2_ref, lk2_ref,
                 g_ref, o_ref, k_all, vt_ext, s_buf, m_buf):
    k_all[0:CTX, :] = kc_ref[...]
    k_all[CTX:, :] = kl_ref[...]
    vt_ext[0:128, 0:CTX] = vc_ref[...].astype(F32).T.astype(BF16)
    vt_ext[0:128, CTX:] = vl_ref[...].astype(F32).T.astype(BF16)
    vt_ext[128:, :] = jnp.ones((DIFF_SUM_ROWS, CTX + N), BF16)

    lam = _diff_lambda(lq1_ref, lk1_ref, lq2_ref, lk2_ref, lam_init)
    g_out = g_ref[...] * (1.0 - lam_init)
    lo = _lane_lo()
    nq = N // TQ

    def stage(qk_tile, qk_slot, pv_slot):
        if qk_slot is not None:
            q = q_ref[pl.ds(pl.multiple_of(qk_tile * TQ, TQ), TQ), :]
            qj = jnp.where(lo if qk_slot == 0 else jnp.logical_not(lo), q, jnp.zeros_like(q))
            m8 = None
        if pv_slot is not None:
            m = jnp.max(m_buf[pv_slot], axis=0, keepdims=True)
            acc = None

        def numerators(c, m_c):
            rows = slice(DIFF_CHUNKS[c], DIFF_CHUNKS[c + 1])
            return jnp.exp2((s_buf[pv_slot, rows, :] - m_c).astype(BF16))

        nck = len(DIFF_CHUNKS) - 1
        p_next = numerators(0, m) if pv_slot is not None else None
        cm_prev = None
        for c in range(nck):
            rows = slice(DIFF_CHUNKS[c], DIFF_CHUNKS[c + 1])
            m_c = m if pv_slot is not None else None
            if qk_slot is not None:
                if pv_slot is not None and cm_prev is not None:
                    m_c = jnp.maximum(m, jnp.minimum(cm_prev[0:1, :], m))
                s = _dot_t(k_all[rows, :], qj)
                s_buf[qk_slot, rows, :] = s
                cm_prev = jnp.max(s.reshape(-1, 8, TQ), axis=0)
                m8 = cm_prev if m8 is None else jnp.maximum(m8, cm_prev)
            if pv_slot is not None:
                p = p_next
                if c + 1 < nck:
                    p_next = numerators(c + 1, m_c)
                d = _dot(vt_ext[:, rows], p)
                acc = d if acc is None else acc + d
        if qk_slot is not None:
            m_buf[qk_slot] = m8
        if pv_slot is not None:
            return acc[0:128, :] / acc[128:129, :]
        return None

    def finish(t, o1, o2):
        o = o1 - lam * o2
        ms = jnp.mean(o * o, axis=0, keepdims=True)
        on = (o * lax.rsqrt(ms + SUBLN_EPS)).T
        o_ref[pl.ds(pl.multiple_of(t * TQ, TQ), TQ), :] = (on * g_out).astype(BF16)

    def tile_body(t, carry):
        o1 = stage(t, 1, 0)
        finish(t, o1, stage(jnp.minimum(t + 1, nq - 1), 0, 1))
        return carry

    stage(0, 0, None)
    lax.fori_loop(0, nq, tile_body, 0, unroll=DIFF_UNROLL)


def _diff_ctx_kernel(lam_init, q_ref, kc_ref, vc_ref, lq1_ref, lk1_ref, lq2_ref, lk2_ref, g_ref, o_ref):
    lam = _diff_lambda(lq1_ref, lk1_ref, lq2_ref, lk2_ref, lam_init)
    lo = _lane_lo()
    for h in range(D // 128):
        head = slice(h * 128, (h + 1) * 128)
        q = q_ref[:, head]
        kc = kc_ref[:, head]
        vc = vc_ref[:, head]
        v_ext = jnp.concatenate([vc, jnp.ones_like(vc)], axis=1)
        outs = []
        for jj in range(2):
            qj = jnp.where(lo if jj == 0 else jnp.logical_not(lo), q, jnp.zeros_like(q))
            o = _softmax_pv(_dot_t(qj, kc), v_ext)
            outs.append(o[:, :128] / o[:, 128:])
        o_ref[:, head] = _diff_finish(outs[0], outs[1], lam, g_ref[...], lam_init)


def _diff(qkv, lams, g, j, lam_init):
    nh = D // 128
    r0 = LAT // CTX
    return pl.pallas_call(
        functools.partial(_diff_kernel, lam_init),
        grid=(B, nh),
        in_specs=[
            pl.BlockSpec((N, 128), lambda b, h: (b, h)),
            pl.BlockSpec((N, 128), lambda b, h: (b, nh + h)),
            pl.BlockSpec((N, 128), lambda b, h: (b, 2 * nh + h)),
            pl.BlockSpec((CTX, 128), lambda b, h: (r0 + b, nh + h)),
            pl.BlockSpec((CTX, 128), lambda b, h: (r0 + b, 2 * nh + h)),
        ] + [_layer_spec((1, HD), j)] * 4 + [_layer_spec((1, 2 * HD), j)],
        out_specs=pl.BlockSpec((N, 128), lambda b, h: (b, h)),
        out_shape=jax.ShapeDtypeStruct((LAT, D), BF16),
        scratch_shapes=[
            pltpu.VMEM((CTX + N, 128), BF16),
            pltpu.VMEM((128 + DIFF_SUM_ROWS, CTX + N), BF16),
            pltpu.VMEM((2, CTX + N, TQ), F32),
            pltpu.VMEM((2, 8, TQ), F32),
        ],
        compiler_params=_cparams(("arbitrary", "arbitrary")),
        name="diff_attn",
    )(qkv, qkv, qkv, qkv, qkv, *lams, g)


def _diff_ctx(qkv, lams, g, j, lam_init):
    r0 = LAT // CTX
    return pl.pallas_call(
        functools.partial(_diff_ctx_kernel, lam_init),
        grid=(B,),
        in_specs=[
            pl.BlockSpec((CTX, D), lambda b: (r0 + b, 0)),
            pl.BlockSpec((CTX, D), lambda b: (r0 + b, 1)),
            pl.BlockSpec((CTX, D), lambda b: (r0 + b, 2)),
        ] + [_layer_spec((1, HD), j)] * 4 + [_layer_spec((1, 2 * HD), j)],
        out_specs=pl.BlockSpec((CTX, D), lambda b: (b, 0)),
        out_shape=jax.ShapeDtypeStruct((B * CTX, D), BF16),
        compiler_params=_cparams(("arbitrary",)),
        name="diff_attn_ctx",
    )(qkv, qkv, qkv, *lams, g)


def _post_kernel(last, n_x, widths, *refs):
    n_att = len(widths) * (1 if last else 2)
    x_refs = refs[:n_x]
    att_refs = refs[n_x:n_x + n_att]
    mod_ref, g_ref, wo_ref, wi_ref, w2_ref, fg_ref, o_ref = refs[n_x + n_att:]
    x = _load_rows(x_refs)
    is_latent = pl.program_id(0) < NT_LAT
    y = None
    off = 0
    for i, w in enumerate(widths):
        if last:
            a = att_refs[i][...]
        else:
            a = jnp.where(is_latent, att_refs[2 * i][...], att_refs[2 * i + 1][...])
        d = _dot(a, wo_ref[off:off + w, :])
        y = d if y is None else y + d
        off += w
    x2 = x + mod_ref[2:3, :] * y
    h = _norm_mod(x2, g_ref[...], mod_ref[3:4, :], mod_ref[4:5, :]).astype(BF16)
    acc = jnp.zeros((TM, D), F32)
    for c in range(MLP_H // D):
        u = jnp.maximum(_dot(h, wi_ref[:, c * D:(c + 1) * D]), 0.0)
        acc = acc + _dot((u * u).astype(BF16), w2_ref[c * D:(c + 1) * D, :])
    x3 = x2 + mod_ref[5:6, :] * acc
    if last:
        ms = jnp.mean(x3 * x3, axis=-1, keepdims=True)
        x3 = x3 * lax.rsqrt(ms + NORM_EPS) * fg_ref[...]
    o_ref[...] = x3


def _post(xs, att, mods, i, g_all, wo_all, j, wi_all, w2_all, fg, last):
    nt = NT_LAT if last else NT
    x_specs, x_args = _x_specs(xs)
    widths = tuple(a_lat.shape[1] for a_lat, _ in att)
    assert sum(widths) == D
    att_specs, att_args = [], []
    for (a_lat, a_ctx), w in zip(att, widths):
        att_specs.append(pl.BlockSpec((TM, w), lambda t: (jnp.minimum(t, NT_LAT - 1), 0)))
        att_args.append(a_lat)
        if not last:
            att_specs.append(pl.BlockSpec((TM, w), lambda t: (jnp.maximum(t - NT_LAT, 0), 0)))
            att_args.append(a_ctx)
    once = dict(pipeline_mode=pl.Buffered(1))
    return pl.pallas_call(
        functools.partial(_post_kernel, last, len(x_args), widths),
        grid=(nt,),
        in_specs=x_specs + att_specs + [
            _mod_spec(i),
            _layer_spec((1, D), i),
            _layer_spec((D, D), j, **once),
            _layer_spec((D, MLP_H), i, **once),
            _layer_spec((MLP_H, D), i, **once),
            pl.BlockSpec((1, D), lambda t: (0, 0)),
        ],
        out_specs=pl.BlockSpec((TM, D), lambda t: (t, 0)),
        out_shape=jax.ShapeDtypeStruct((nt * TM, D), F32),
        compiler_params=_cparams(("arbitrary",)),
        name="post_last" if last else "post",
    )(*x_args, *att_args, mods, g_all, wo_all, wi_all, w2_all, fg)


def _channel_dft():
    n = np.arange(FD)
    ang = 2.0 * np.pi * ((n[:, None] * n[None, :]) % FD) / FD
    m = np.concatenate([np.cos(ang), np.sin(ang)], axis=1) / math.sqrt(FD)
    return jnp.asarray(m.astype(np.float32)).astype(BF16)


def _ctx_dft():
    n = np.arange(CTX)
    ang = 2.0 * np.pi * ((n[:, None] * n[None, :]) % CTX) / CTX
    m = np.concatenate([np.cos(ang), np.sin(ang)], axis=1) / math.sqrt(CTX)
    return jnp.asarray(m.astype(np.float32)).astype(BF16)


def _row_reversal():
    m = np.zeros((REV_BLK, 2 * REV_BLK), np.float32)
    m[np.arange(REV_BLK), REV_BLK - np.arange(REV_BLK)] = 1.0
    return jnp.asarray(m).astype(BF16)


def _position_dft_tables():
    half = N // 2
    l = jnp.arange(half, dtype=jnp.int32)[None, :]
    dk = jnp.arange(TMF, dtype=jnp.int32)[:, None]
    ang = ((dk * l) % N).astype(F32) * (2.0 * math.pi / N)
    k0 = (jnp.arange(N // TMF, dtype=jnp.int32) * TMF)[:, None]
    ang0 = ((k0 * l) % N).astype(F32) * (2.0 * math.pi / N)
    scale = 1.0 / math.sqrt(N)
    ca = (jnp.cos(ang0) * scale).reshape(N // TMF, 1, half)
    sa = (jnp.sin(ang0) * scale).reshape(N // TMF, 1, half)
    return jnp.cos(ang), jnp.sin(ang), ca, sa


def _rope_tables():
    t = jnp.arange(N)
    rows, cols = t // GW, t % GW
    nf = HD // 4
    inv_freq = ROPE_THETA ** (-jnp.arange(nf, dtype=F32) / nf)
    lane = np.arange(128)
    d = lane % HD
    use_col = d >= HD // 2
    first = (d % (HD // 2)) < nf
    pos = jnp.where(use_col[None, :], cols[:, None], rows[:, None]).astype(F32)
    ang = pos * inv_freq[d % nf][None, :]
    cos, sin = jnp.cos(ang), jnp.sin(ang)
    sa = jnp.where(first[None, :], -sin, 0.0)
    sb = jnp.where(first[None, :], 0.0, sin)
    pad = jnp.zeros((TM, 128), F32)
    return (jnp.concatenate([cos, pad + 1.0]), jnp.concatenate([sa, pad]), jnp.concatenate([sb, pad]))


def kernel(x, c, ctx, c_ctx, w_mod, b_mod, norm_mix_g, norm_mlp_g, w_mlp_in, w_mlp_out, w_in_ab, w_out_ab,
           na_rpb, w_qkv_diff, w_out_diff, diff_lq1, diff_lk1, diff_lq2, diff_lk2, diff_subln_g,
           final_norm_g):
    assert x.shape == (B, N, D) and ctx.shape == (B, CTX, D)
    xs = (x.reshape(LAT, D), ctx.reshape(B * CTX, D))
    cin = jnp.concatenate([c, c_ctx[None, :], jnp.zeros((8 - B - 1, D), F32)], axis=0)
    mods = _modulation(cin, w_mod, b_mod).reshape(DEPTH, 8, 6, D)

    cs = _channel_dft()
    mctx = _ctx_dft()
    rev = _row_reversal()
    tc, ts, ca, sa = _position_dft_tables()
    rope = _rope_tables()
    bias = _na_bias(na_rpb.reshape(-1))
    fg = final_norm_g.reshape(1, D)

    g_mix = norm_mix_g.reshape(DEPTH, 1, D)
    g_mlp = norm_mlp_g.reshape(DEPTH, 1, D)
    w_ab, wo_ab = w_in_ab.astype(BF16), w_out_ab.astype(BF16)
    w_qkv, wo_diff = w_qkv_diff.astype(BF16), w_out_diff.astype(BF16)
    w_in, w_out = w_mlp_in.astype(BF16), w_mlp_out.astype(BF16)
    lams = [v.reshape(-1, 1, HD) for v in (diff_lq1, diff_lk1, diff_lq2, diff_lk2)]
    g_sub = diff_subln_g.reshape(-1, 1, 2 * HD)

    for i in range(DEPTH):
        last = i == DEPTH - 1
        j = i // 2
        if i % 2 == 0:
            fcs, qkv = _proj_even(xs, mods, i, g_mix, w_ab, j, cs)
            att = [(_fourier(fcs, rev, tc, ts, ca, sa), None if last else _fourier_ctx(fcs, mctx)),
                   (_na(qkv, bias, j), None if last else _na_ctx(qkv))]
            wo = wo_ab
        else:
            lam_init = 0.8 - 0.6 * math.exp(-0.3 * i)
            qkv = _proj_odd(xs, mods, i, g_mix, w_qkv, j, *rope)
            att = [(_diff(qkv, lams, g_sub, j, lam_init),
                    None if last else _diff_ctx(qkv, lams, g_sub, j, lam_init))]
            wo = wo_diff
        xs = _post(xs, att, mods, i, g_mlp, wo, j, w_in, w_out, fg, last)
    return xs.reshape(B, N, D)
```
